```python
import functools
import jax, jax.numpy as jnp
from jax import lax
import numpy as np

D_MODEL = 2048
BATCH = 2
SEQ = 4096
DEPTH = 4
DEC_BATCH = 8
DEC_SEQ = 8
PAST_LEN = 16384
PAGE_SIZE = 128

HEAD_DIM = 128
MIX_WIDTH = D_MODEL
CONV_CH = MIX_WIDTH // 4
GDN_HEADS = (3 * MIX_WIDTH // 8) // HEAD_DIM
MOBA_HEADS = (MIX_WIDTH - CONV_CH) // HEAD_DIM - GDN_HEADS
GDN_WIDTH = GDN_HEADS * HEAD_DIM
MOBA_WIDTH = MOBA_HEADS * HEAD_DIM
GDN_CONV = 4
GDN_CHUNK = 64
CONV_WIDTH = 31
MOBA_BLOCK = 256
MOBA_TOPK = 3
MOBA_QBLOCK = 64
ROPE_THETA = 10000.0
FFN_HIDDEN = ((8 * D_MODEL + 3 * 256 - 1) // (3 * 256)) * 256
IN_SIZES = (3 * GDN_WIDTH, GDN_WIDTH, GDN_HEADS, GDN_HEADS, 2 * CONV_CH, MOBA_WIDTH, MOBA_WIDTH, MOBA_WIDTH)
N_IN = 4 * GDN_WIDTH + 2 * GDN_HEADS + 2 * CONV_CH + 3 * MOBA_WIDTH
EPS = 1e-6
NEG_INF = -1e30

kernel_name = 'hymba_gdn_conformer_moba_step'

F32 = jnp.float32


def rms_norm(x, g):
    xf = x.astype(F32)
    y = xf * lax.rsqrt(jnp.mean(xf * xf, axis=-1, keepdims=True) + EPS)
    return (y * g.astype(F32)).astype(x.dtype)


def layer_norm(x, g, b):
    xf = x.astype(F32)
    xc = xf - jnp.mean(xf, axis=-1, keepdims=True)
    y = xc * lax.rsqrt(jnp.mean(xc * xc, axis=-1, keepdims=True) + EPS)
    return (y * g.astype(F32) + b.astype(F32)).astype(x.dtype)


def l2_norm(x):
    xf = x.astype(F32)
    return (xf * lax.rsqrt(jnp.sum(xf * xf, axis=-1, keepdims=True) + EPS)).astype(x.dtype)


def rope(x, pos):
    half = HEAD_DIM // 2
    inv_freq = ROPE_THETA ** (-jnp.arange(half, dtype=F32) / half)
    ang = pos.astype(F32)[:, None] * inv_freq[None, :]
    cos = jnp.cos(ang)[:, None, :]
    sin = jnp.sin(ang)[:, None, :]
    xf = x.astype(F32)
    x1, x2 = xf[..., :half], xf[..., half:]
    return jnp.concatenate([x1 * cos - x2 * sin, x2 * cos + x1 * sin], axis=-1).astype(x.dtype)


def causal_depthwise_conv(x_ext, w):
    return lax.conv_general_dilated(
        x_ext, w[:, None, :].astype(x_ext.dtype), window_strides=(1,), padding='VALID',
        dimension_numbers=('NWC', 'WIO', 'NWC'), feature_group_count=x_ext.shape[-1])


def gated_delta_chunked(q, k, v, g, beta, s0):
    B, T, H, DK = q.shape
    C = GDN_CHUNK
    pad = (-T) % C

    def prep(a):
        a = a.astype(F32)
        a = jnp.pad(a, [(0, 0), (0, pad)] + [(0, 0)] * (a.ndim - 2))
        a = a.reshape((B, a.shape[1] // C, C) + a.shape[2:])
        return jnp.moveaxis(a, 3, 1)

    qc = prep(q) * (DK ** -0.5)
    kc = prep(k)
    vc = prep(v)
    gc = jnp.cumsum(prep(g), axis=-1)
    bc = prep(beta)
    incl = jnp.tril(jnp.ones((C, C), bool))
    strict = jnp.tril(jnp.ones((C, C), bool), -1)
    diff = gc[..., :, None] - gc[..., None, :]
    decay = jnp.where(incl, jnp.exp(jnp.where(incl, diff, 0.0)), 0.0)
    kb = kc * bc[..., None]
    lower = jnp.where(strict, jnp.einsum('bhncd,bhnsd->bhncs', kb, kc) * decay, 0.0)
    eye = jnp.eye(C, dtype=F32)
    tmat = lax.linalg.triangular_solve(eye + lower, jnp.broadcast_to(eye, lower.shape),
                                       left_side=True, lower=True, unit_diagonal=True)
    u = jnp.einsum('bhncs,bhnsd->bhncd', tmat, vc * bc[..., None])
    w = jnp.einsum('bhncs,bhnsd->bhncd', tmat, kb * jnp.exp(gc)[..., None])

    def step(state, xs):
        q_i, k_i, u_i, w_i, g_i, d_i = xs
        v_new = u_i - jnp.einsum('bhcd,bhde->bhce', w_i, state)
        intra = jnp.einsum('bhcd,bhsd->bhcs', q_i, k_i) * d_i
        o = (jnp.einsum('bhcd,bhde->bhce', q_i * jnp.exp(g_i)[..., None], state)
             + jnp.einsum('bhcs,bhse->bhce', intra, v_new))
        g_last = g_i[..., -1:]
        state = (state * jnp.exp(g_last)[..., None]
                 + jnp.einsum('bhcd,bhce->bhde', k_i * jnp.exp(g_last - g_i)[..., None], v_new))
        return state, o

    xs = tuple(jnp.moveaxis(a, 2, 0) for a in (qc, kc, u, w, gc, decay))
    state, o = lax.scan(step, s0.astype(F32), xs)
    o = jnp.moveaxis(jnp.moveaxis(o, 0, 2), 1, 3)
    o = o.reshape(B, -1, H, o.shape[-1])[:, :T]
    return o.astype(v.dtype), state


def gdn_mixer(qkv_pre, z, b_raw, a_raw, hist, s0, conv_w, a_log, dt_bias, out_g):
    B, T, _ = qkv_pre.shape
    x_ext = jnp.concatenate([hist.astype(qkv_pre.dtype), qkv_pre], axis=1)
    new_hist = x_ext[:, -(GDN_CONV - 1):]
    qkv = jax.nn.silu(causal_depthwise_conv(x_ext, conv_w))
    q, k, v = [a.reshape(B, T, GDN_HEADS, HEAD_DIM) for a in jnp.split(qkv, 3, axis=-1)]
    q, k = l2_norm(q), l2_norm(k)
    beta = jax.nn.sigmoid(b_raw.astype(F32))
    g = -jnp.exp(a_log.astype(F32)) * jax.nn.softplus(a_raw.astype(F32) + dt_bias.astype(F32))
    o, state = gated_delta_chunked(q, k, v, g, beta, s0)
    o = rms_norm(o, out_g) * jax.nn.silu(z.reshape(B, T, GDN_HEADS, HEAD_DIM))
    return o.reshape(B, T, GDN_WIDTH), state, new_hist


def conformer_conv(glu_in, hist, dw_w, dw_b, ln_g, ln_b):
    u = glu_in[..., :CONV_CH] * jax.nn.sigmoid(glu_in[..., CONV_CH:])
    x_ext = jnp.concatenate([hist.astype(u.dtype), u], axis=1)
    new_hist = x_ext[:, -(CONV_WIDTH - 1):]
    y = causal_depthwise_conv(x_ext, dw_w) + dw_b
    return jax.nn.silu(layer_norm(y, ln_g, ln_b)), new_hist


def key_blocks(k):
    B, L, H, Dh = k.shape
    pad = (-L) % MOBA_BLOCK
    kb = jnp.pad(k, ((0, 0), (0, pad), (0, 0), (0, 0))).reshape(B, -1, MOBA_BLOCK, H, Dh)
    return kb.transpose(0, 3, 1, 2, 4)


def moba_attend(q, pos, kb, vb, km):
    H, NB = kb.shape[:2]
    Tq = q.shape[0]
    kk = min(MOBA_TOPK, NB)
    own = pos // MOBA_BLOCK
    gate = jnp.einsum('qhd,hnd->qhn', q.astype(F32), km)
    gate = jnp.where(jnp.arange(NB)[None, None, :] < own[:, None, None], gate, NEG_INF)
    _, top = lax.top_k(gate, kk)
    sel = jnp.concatenate([top, jnp.broadcast_to(own[:, None, None], (Tq, H, 1)).astype(top.dtype)], axis=-1)
    hidx = jnp.arange(H)[None, :, None]
    kg = kb[hidx, sel]
    vg = vb[hidx, sel]
    s = jnp.einsum('qhd,qhjkd->qhjk', q, kg, preferred_element_type=F32) * (HEAD_DIM ** -0.5)
    rank = jnp.arange(kk + 1)[None, None, :, None]
    keypos = sel[..., None] * MOBA_BLOCK + jnp.arange(MOBA_BLOCK)
    ok = jnp.where(rank < kk, rank < own[:, None, None, None], keypos <= pos[:, None, None, None])
    s = jnp.where(ok, s, NEG_INF)
    p = jax.nn.softmax(s.reshape(Tq, H, -1), axis=-1).reshape(s.shape)
    return jnp.einsum('qhjk,qhjkd->qhd', p.astype(vg.dtype), vg)


def moba_prompt(q, k, v):
    B, S, H, Dh = q.shape
    kb, vb = key_blocks(k), key_blocks(v)
    km = jnp.mean(kb.astype(F32), axis=3)
    nq = S // MOBA_QBLOCK
    qb = q.reshape(B, nq, MOBA_QBLOCK, H, Dh).swapaxes(0, 1)
    pos = jnp.arange(S, dtype=jnp.int32).reshape(nq, MOBA_QBLOCK)
    attend = jax.vmap(moba_attend, in_axes=(0, None, 0, 0, 0))
    out = lax.map(lambda a: attend(a[0], a[1], kb, vb, km), (qb, pos))
    return out.swapaxes(0, 1).reshape(B, S, H, Dh)


def moba_sample(q, k, v, k_past, v_past, past_len):
    T = q.shape[1]
    kb = key_blocks(jnp.concatenate([k_past.astype(k.dtype), k], axis=1))
    vb = key_blocks(jnp.concatenate([v_past.astype(v.dtype), v], axis=1))
    km = jnp.mean(kb.astype(F32), axis=3)
    pos = past_len + jnp.arange(T, dtype=jnp.int32)
    return jax.vmap(moba_attend, in_axes=(0, None, 0, 0, 0))(q, pos, kb, vb, km)


def layer_forward(x, pos, gdn_hist, gdn_s0, conv_hist, attend,
                  norm_mix_g, w_in, gdn_conv_w, gdn_a_log, gdn_dt_bias, gdn_out_g,
                  conv_dw_w, conv_dw_b, conv_ln_g, conv_ln_b, moba_q_g, moba_k_g,
                  w_out, norm_ffn_g, w_gate, w_up, w_down):
    B, T, _ = x.shape
    xn = rms_norm(x, norm_mix_g)
    proj = xn @ w_in
    cuts = np.cumsum(IN_SIZES)[:-1].tolist()
    qkv_a, z_a, b_a, a_a, glu_b, q_c, k_c, v_c = jnp.split(proj, cuts, axis=-1)
    o_a, s_a, h_a = gdn_mixer(qkv_a, z_a, b_a, a_a, gdn_hist, gdn_s0, gdn_conv_w, gdn_a_log, gdn_dt_bias, gdn_out_g)
    o_b, h_b = conformer_conv(glu_b, conv_hist, conv_dw_w, conv_dw_b, conv_ln_g, conv_ln_b)
    qh = rope(rms_norm(q_c.reshape(B, T, MOBA_HEADS, HEAD_DIM), moba_q_g), pos)
    kh = rope(rms_norm(k_c.reshape(B, T, MOBA_HEADS, HEAD_DIM), moba_k_g), pos)
    vh = v_c.reshape(B, T, MOBA_HEADS, HEAD_DIM)
    o_c = attend(qh, kh, vh).reshape(B, T, MOBA_WIDTH)
    h = x + jnp.concatenate([o_a, o_b, o_c], axis=-1) @ w_out
    hn = rms_norm(h, norm_ffn_g)
    y = h + (jax.nn.silu(hn @ w_gate) * (hn @ w_up)) @ w_down
    return y, kh, vh, s_a, h_a, h_b


def setup_inputs(seed: int = 0) -> dict:
    key = jax.random.key(seed)
    ks = jax.random.split(key, 26)
    nrm = jax.random.normal
    n_pages = PAST_LEN // PAGE_SIZE
    n_used = DEC_BATCH * n_pages
    n_pool = n_used + max(1, n_used // 4)
    perm = jax.random.permutation(ks[0], n_pool)
    page_table = perm[:n_used].reshape(DEC_BATCH, n_pages).astype(jnp.int32)
    dt = jnp.exp(jax.random.uniform(ks[12], (DEPTH, GDN_HEADS), F32, jnp.log(1e-3), jnp.log(1e-1)))
    return {
        'x_prompt': nrm(ks[1], (BATCH, SEQ, D_MODEL), F32),
        'x_sample': nrm(ks[2], (DEC_BATCH, DEC_SEQ, D_MODEL), F32),
        'cache_k': nrm(ks[3], (DEPTH, n_pool, PAGE_SIZE, MOBA_HEADS, HEAD_DIM), F32),
        'cache_v': nrm(ks[4], (DEPTH, n_pool, PAGE_SIZE, MOBA_HEADS, HEAD_DIM), F32),
        'state_gdn': nrm(ks[5], (DEPTH, DEC_BATCH, GDN_HEADS, HEAD_DIM, HEAD_DIM), F32) * HEAD_DIM ** -0.5,
        'state_gdn_conv': nrm(ks[6], (DEPTH, DEC_BATCH, GDN_CONV - 1, 3 * GDN_WIDTH), F32),
        'state_conv': nrm(ks[7], (DEPTH, DEC_BATCH, CONV_WIDTH - 1, CONV_CH), F32) * 0.5,
        'page_table': page_table,
        'norm_mix': 1.0 + 0.02 * nrm(ks[8], (DEPTH, D_MODEL), F32),
        'w_in': nrm(ks[9], (DEPTH, D_MODEL, N_IN), F32) * D_MODEL ** -0.5,
        'gdn_conv_w': nrm(ks[10], (DEPTH, GDN_CONV, 3 * GDN_WIDTH), F32) * GDN_CONV ** -0.5,
        'gdn_a_log': jnp.log(jax.random.uniform(ks[11], (DEPTH, GDN_HEADS), F32, 1.0, 16.0)),
        'gdn_dt_bias': jnp.log(jnp.expm1(dt)),
        'gdn_out_norm': 1.0 + 0.02 * nrm(ks[13], (DEPTH, HEAD_DIM), F32),
        'conv_dw_w': nrm(ks[14], (DEPTH, CONV_WIDTH, CONV_CH), F32) * CONV_WIDTH ** -0.5,
        'conv_dw_b': 0.02 * nrm(ks[15], (DEPTH, CONV_CH), F32),
        'conv_ln_g': 1.0 + 0.02 * nrm(ks[16], (DEPTH, CONV_CH), F32),
        'conv_ln_b': 0.02 * nrm(ks[17], (DEPTH, CONV_CH), F32),
        'moba_q_norm': 1.0 + 0.02 * nrm(ks[18], (DEPTH, HEAD_DIM), F32),
        'moba_k_norm': 1.0 + 0.02 * nrm(ks[19], (DEPTH, HEAD_DIM), F32),
        'w_out': nrm(ks[20], (DEPTH, MIX_WIDTH, D_MODEL), F32) * MIX_WIDTH ** -0.5,
        'norm_ffn': 1.0 + 0.02 * nrm(ks[21], (DEPTH, D_MODEL), F32),
        'w_gate': nrm(ks[22], (DEPTH, D_MODEL, FFN_HIDDEN), F32) * D_MODEL ** -0.5,
        'w_up': nrm(ks[23], (DEPTH, D_MODEL, FFN_HIDDEN), F32) * D_MODEL ** -0.5,
        'w_down': nrm(ks[24], (DEPTH, FFN_HIDDEN, D_MODEL), F32) * FFN_HIDDEN ** -0.5,
    }


def reference(x_prompt, x_sample, cache_k, cache_v, state_gdn, state_gdn_conv, state_conv, page_table,
              norm_mix, w_in, gdn_conv_w, gdn_a_log, gdn_dt_bias, gdn_out_norm,
              conv_dw_w, conv_dw_b, conv_ln_g, conv_ln_b, moba_q_norm, moba_k_norm,
              w_out, norm_ffn, w_gate, w_up, w_down):
    B, S, _ = x_prompt.shape
    DB, T, _ = x_sample.shape
    n_pages = page_table.shape[1]
    past_len = n_pages * PAGE_SIZE
    pos_p = jnp.arange(S, dtype=jnp.int32)
    pos_s = past_len + jnp.arange(T, dtype=jnp.int32)
    gdn_hist0 = jnp.zeros((B, GDN_CONV - 1, 3 * GDN_WIDTH), x_prompt.dtype)
    gdn_state0 = jnp.zeros((B, GDN_HEADS, HEAD_DIM, HEAD_DIM), F32)
    conv_hist0 = jnp.zeros((B, CONV_WIDTH - 1, CONV_CH), x_prompt.dtype)
    yp, ys = x_prompt, x_sample
    kp_l, vp_l, ks_l, vs_l, sp_l, ss_l, gp_l, gs_l, cp_l, cs_l = [], [], [], [], [], [], [], [], [], []
    for l in range(DEPTH):
        lw = (norm_mix[l], w_in[l], gdn_conv_w[l], gdn_a_log[l], gdn_dt_bias[l], gdn_out_norm[l],
              conv_dw_w[l], conv_dw_b[l], conv_ln_g[l], conv_ln_b[l], moba_q_norm[l], moba_k_norm[l],
              w_out[l], norm_ffn[l], w_gate[l], w_up[l], w_down[l])
        yp, kp, vp, sp, gp, cp = layer_forward(yp, pos_p, gdn_hist0, gdn_state0, conv_hist0, moba_prompt, *lw)
        k_past = cache_k[l][page_table].reshape(DB, past_len, MOBA_HEADS, HEAD_DIM)
        v_past = cache_v[l][page_table].reshape(DB, past_len, MOBA_HEADS, HEAD_DIM)
        attend_s = functools.partial(moba_sample, k_past=k_past, v_past=v_past, past_len=past_len)
        ys, kn, vn, sn, gn, cn = layer_forward(ys, pos_s, state_gdn_conv[l], state_gdn[l], state_conv[l], attend_s, *lw)
        kp_l.append(kp); vp_l.append(vp); ks_l.append(kn); vs_l.append(vn)
        sp_l.append(sp); ss_l.append(sn); gp_l.append(gp); gs_l.append(gn); cp_l.append(cp); cs_l.append(cn)
    return (yp, ys, jnp.stack(kp_l), jnp.stack(vp_l), jnp.stack(ks_l), jnp.stack(vs_l),
            jnp.stack(sp_l), jnp.stack(ss_l), jnp.stack(gp_l), jnp.stack(gs_l), jnp.stack(cp_l), jnp.stack(cs_l))
```

```python
import functools

import jax
import jax.numpy as jnp
from jax import lax
from jax.experimental import pallas as pl
from jax.experimental.pallas import tpu as pltpu

F32 = jnp.float32
BF16 = jnp.bfloat16
I32 = jnp.int32

EPS = 1e-6
NEG_INF = -1e30
HEAD_DIM = 128
MOBA_BLOCK = 256
MOBA_TOPK = 3
ROPE_THETA = 10000.0
LANES = 128
SUBLANES = 8
BA_WIDTH = 2 * LANES
VMEM_LIMIT = 60 * 1024 * 1024


def _cp(sem, vmem=VMEM_LIMIT):
    return pltpu.CompilerParams(dimension_semantics=sem, vmem_limit_bytes=vmem)


def _bdot(a, b):
    return jnp.dot(a.astype(BF16), b.astype(BF16), preferred_element_type=F32)


def _nt(a, b):
    return lax.dot_general(a, b, (((1,), (1,)), ((), ())), preferred_element_type=F32)


def _bdot_nt(a, b):
    return _nt(a.astype(BF16), b.astype(BF16))


def _split_bf16(a):
    hi = a.astype(BF16)
    lo = (a - hi.astype(F32)).astype(BF16)
    return hi, lo


def _dot3_nt(a, b):
    ah, al = _split_bf16(a)
    bh, bl = _split_bf16(b)
    return _nt(ah, bh) + (_nt(ah, bl) + _nt(al, bh))


def _sigmoid(x):
    return 1.0 / (1.0 + jnp.exp(-x))


def _silu(x):
    return x * _sigmoid(x)


def _softplus(x):
    return jnp.maximum(x, 0.0) + jnp.log(1.0 + jnp.exp(-jnp.abs(x)))


def _rms(x, g):
    return x * lax.rsqrt(jnp.mean(x * x, axis=-1, keepdims=True) + EPS) * g


def _rope(x, cos_full, sin_signed):
    return x * cos_full + pltpu.roll(x, HEAD_DIM // 2, 1) * sin_signed


def _largest_divisor(n, cap, mult):
    best = None
    for d in range(mult, min(n, cap) + 1, mult):
        if n % d == 0:
            best = d
    assert best is not None, (n, cap, mult)
    return best


def _inproj_body(x_ref, g_ref, w_ref, o_ref, xn_ref):
    @pl.when(pl.program_id(1) == 0)
    def _():
        xn_ref[...] = _rms(x_ref[...], g_ref[...]).astype(BF16)

    o_ref[...] = jnp.dot(xn_ref[...], w_ref[...], preferred_element_type=F32)


def _inproj(x, g, w, l, tm, tn):
    M, D = x.shape
    NP = w.shape[-1]
    return pl.pallas_call(
        _inproj_body,
        grid=(M // tm, NP // tn),
        in_specs=[pl.BlockSpec((tm, D), lambda i, j: (i, 0)),
                  pl.BlockSpec((None, 1, D), lambda i, j: (l, 0, 0)),
                  pl.BlockSpec((None, D, tn), lambda i, j: (l, 0, j))],
        out_specs=pl.BlockSpec((tm, tn), lambda i, j: (i, j)),
        out_shape=jax.ShapeDtypeStruct((M, NP), F32),
        scratch_shapes=[pltpu.VMEM((tm, D), BF16)],
        compiler_params=_cp(("parallel", "arbitrary")),
        name="inproj",
    )(x, g, w)


def _ffn_body(x_ref, oa_ref, ob_ref, oc_ref, wa_ref, wb_ref, wc_ref, g_ref, wg_ref, wu_ref, wd_ref,
              y_ref, hn_ref):
    @pl.when(pl.program_id(1) == 0)
    def _():
        h = (x_ref[...]
             + jnp.dot(oa_ref[...], wa_ref[...], preferred_element_type=F32)
             + jnp.dot(ob_ref[...], wb_ref[...], preferred_element_type=F32)
             + jnp.dot(oc_ref[...], wc_ref[...], preferred_element_type=F32))
        y_ref[...] = h
        hn_ref[...] = _rms(h, g_ref[...]).astype(BF16)

    hn = hn_ref[...]
    a = jnp.dot(hn, wg_ref[...], preferred_element_type=F32)
    b = jnp.dot(hn, wu_ref[...], preferred_element_type=F32)
    t = (_silu(a) * b).astype(BF16)
    y_ref[...] += jnp.dot(t, wd_ref[...], preferred_element_type=F32)


def _outproj_ffn(x, oa, ob, oc, wa, wb, wc, g, wg, wu, wd, l, tm, th):
    M, D = x.shape
    FH = wg.shape[-1]
    GW, CC, MW = oa.shape[1], ob.shape[1], oc.shape[1]
    row = lambda i, j: (i, 0)
    return pl.pallas_call(
        _ffn_body,
        grid=(M // tm, FH // th),
        in_specs=[pl.BlockSpec((tm, D), row),
                  pl.BlockSpec((tm, GW), row),
                  pl.BlockSpec((tm, CC), row),
                  pl.BlockSpec((tm, MW), row),
                  pl.BlockSpec((None, GW, D), lambda i, j: (l, 0, 0), pipeline_mode=pl.Buffered(1)),
                  pl.BlockSpec((None, CC, D), lambda i, j: (l, 0, 0), pipeline_mode=pl.Buffered(1)),
                  pl.BlockSpec((None, MW, D), lambda i, j: (l, 0, 0), pipeline_mode=pl.Buffered(1)),
                  pl.BlockSpec((None, 1, D), lambda i, j: (l, 0, 0)),
                  pl.BlockSpec((None, D, th), lambda i, j: (l, 0, j)),
                  pl.BlockSpec((None, D, th), lambda i, j: (l, 0, j)),
                  pl.BlockSpec((None, th, D), lambda i, j: (l, j, 0))],
        out_specs=pl.BlockSpec((tm, D), row),
        out_shape=jax.ShapeDtypeStruct((M, D), F32),
        scratch_shapes=[pltpu.VMEM((tm, D), BF16)],
        compiler_params=_cp(("parallel", "arbitrary")),
        name="outproj_ffn",
    )(x, oa, ob, oc, wa, wb, wc, g, wg, wu, wd)


def _gdn_chunk(q, k, v, beta, g, S):
    C = q.shape[0]
    ri = lax.broadcasted_iota(I32, (C, C), 0)
    ci = lax.broadcasted_iota(I32, (C, C), 1)
    incl = ri >= ci
    strict = ri > ci
    eye = ri == ci
    g_row = jnp.sum(jnp.where(eye, g, 0.0), axis=0, keepdims=True)
    gc_row = jnp.sum(jnp.where(ri <= ci, g, 0.0), axis=0, keepdims=True)
    gc_col = jnp.sum(jnp.where(incl, g_row, 0.0), axis=1, keepdims=True)
    decay = jnp.where(incl, jnp.exp(jnp.where(incl, gc_col - gc_row, 0.0)), 0.0)
    kb = k * beta
    a = jnp.where(strict, _bdot_nt(kb, k) * decay, 0.0)
    assert C & (C - 1) == 0
    xor = ri ^ ci
    p = jnp.where(eye, 1.0, 0.0) - jnp.where(xor == 1, a, 0.0)
    for lvl in range(1, C.bit_length() - 1):
        a_off = jnp.where((xor >> lvl) == 1, a, 0.0)
        p = p - _bdot(p, _bdot(a_off, p))
    egc = jnp.exp(gc_col)
    u = _bdot(p, v * beta)
    w = _bdot(p, kb * egc)
    v_new = u - _bdot(w, S)
    qs = q * (q.shape[1] ** -0.5)
    intra = jnp.where(incl, _bdot_nt(qs, k) * decay, 0.0)
    o = _bdot(qs * egc, S) + _bdot(intra, v_new)
    g_last = gc_col[C - 1:C, :]
    kdec = k * jnp.exp(g_last - gc_col)
    s_new = S * jnp.exp(g_last) + _bdot(kdec.T, v_new)
    return o, s_new


def _gdn_body(q_ref, k_ref, v_ref, z_ref, ba_ref, hq_ref, hk_ref, hv_ref, s0_ref,
              wq_ref, wk_ref, wv_ref, alog_ref, dtb_ref, og_ref, obuf_hbm,
              o_ref, sout_ref, xq_ref, xk_ref, xv_ref, s_ref, *, Tt, C, HB, H, KW, NT):
    del obuf_hbm
    hb = pl.program_id(1)
    t = pl.program_id(2)
    HR = SUBLANES
    xrefs = (xq_ref, xk_ref, xv_ref)

    @pl.when(t == 0)
    def _():
        for xr, hr in zip(xrefs, (hq_ref, hk_ref, hv_ref)):
            xr[HR - (KW - 1):HR, :] = hr[...]
        s_ref[...] = s0_ref[...]

    for xr, r in zip(xrefs, (q_ref, k_ref, v_ref)):
        xr[HR:HR + Tt, :] = r[...]
        if C > Tt:
            xr[HR + Tt:HR + C, :] = jnp.zeros((C - Tt, xr.shape[1]), F32)

    def conv(xr, w_ref, s):
        acc = None
        for j in range(KW):
            term = xr[HR - (KW - 1) + j:HR - (KW - 1) + j + C, s * LANES:(s + 1) * LANES] \
                * w_ref[j:j + 1, s * LANES:(s + 1) * LANES]
            acc = term if acc is None else acc + term
        return acc

    ba = ba_ref[...]
    if C > Tt:
        ba = jnp.concatenate([ba, jnp.zeros((C - Tt, ba.shape[1]), F32)], axis=0)
    beta_all = _sigmoid(ba[:, :LANES])
    g_all = -jnp.exp(alog_ref[...]) * _softplus(ba[:, LANES:] + dtb_ref[...])
    lane = lax.broadcasted_iota(I32, (1, LANES), 1)
    rvalid = lax.broadcasted_iota(I32, (C, 1), 0) < Tt

    for s in range(HB):
        hsel = lane == hb * HB + s
        beta = jnp.sum(jnp.where(hsel, beta_all, 0.0), axis=1, keepdims=True)
        g = jnp.sum(jnp.where(hsel, g_all, 0.0), axis=1, keepdims=True)
        q = _silu(conv(xq_ref, wq_ref, s))
        k = _silu(conv(xk_ref, wk_ref, s))
        v = _silu(conv(xv_ref, wv_ref, s))
        q = q * lax.rsqrt(jnp.sum(q * q, axis=-1, keepdims=True) + EPS)
        k = k * lax.rsqrt(jnp.sum(k * k, axis=-1, keepdims=True) + EPS)
        if C > Tt:
            q = jnp.where(rvalid, q, 0.0)
            k = jnp.where(rvalid, k, 0.0)
            v = jnp.where(rvalid, v, 0.0)
            beta = jnp.where(rvalid, beta, 0.0)
            g = jnp.where(rvalid, g, 0.0)
        o, s_new = _gdn_chunk(q, k, v, beta, g, s_ref[s])
        s_ref[s] = s_new
        zz = z_ref[:, s * LANES:(s + 1) * LANES]
        o_ref[:, s * LANES:(s + 1) * LANES] = (_rms(o[:Tt], og_ref[...]) * _silu(zz)).astype(o_ref.dtype)

    if NT > 1:
        for xr in xrefs:
            xr[0:HR, :] = xr[C:C + HR, :]

    @pl.when(t == NT - 1)
    def _():
        sout_ref[...] = s_ref[...]


def _gdn(proj, o_buf, hist, s0, conv_w, alog, dtb, og, l, *, lay, row0, nb, T, Tt, C, HB):
    M = proj.shape[0]
    H = s0.shape[1]
    GW = H * HEAD_DIM
    KW = conv_w.shape[1]
    NT = T // Tt
    W = HB * LANES
    assert T % Tt == 0 and row0 % Tt == 0 and H % HB == 0 and (NT == 1 or Tt == C)
    rb0 = row0 // Tt

    def col(off):
        assert off % W == 0
        return lambda b, h, t: (rb0 + b * NT + t, off // W + h)

    def hcol(off):
        return lambda b, h, t: (b, 0, off // W + h)

    def wcol(off):
        return lambda b, h, t: (l, 0, off // W + h)

    vec = lambda b, h, t: (l, 0, 0)
    in_specs = [pl.BlockSpec((Tt, W), col(lay["GQ"])),
                pl.BlockSpec((Tt, W), col(lay["GK"])),
                pl.BlockSpec((Tt, W), col(lay["GV"])),
                pl.BlockSpec((Tt, W), col(lay["GZ"])),
                pl.BlockSpec((Tt, BA_WIDTH), lambda b, h, t: (rb0 + b * NT + t, lay["BA"] // BA_WIDTH)),
                pl.BlockSpec((None, KW - 1, W), hcol(0)),
                pl.BlockSpec((None, KW - 1, W), hcol(GW)),
                pl.BlockSpec((None, KW - 1, W), hcol(2 * GW)),
                pl.BlockSpec((None, HB, HEAD_DIM, HEAD_DIM), lambda b, h, t: (b, h, 0, 0)),
                pl.BlockSpec((None, KW, W), wcol(0)),
                pl.BlockSpec((None, KW, W), wcol(GW)),
                pl.BlockSpec((None, KW, W), wcol(2 * GW)),
                pl.BlockSpec((None, 1, LANES), vec),
                pl.BlockSpec((None, 1, LANES), vec),
                pl.BlockSpec((None, 1, HEAD_DIM), vec),
                pl.BlockSpec(memory_space=pl.ANY)]
    args = [proj, proj, proj, proj, proj, hist, hist, hist, s0, conv_w, conv_w, conv_w, alog, dtb, og, o_buf]
    assert o_buf.shape == (M, GW) and o_buf.dtype == BF16
    o, st = pl.pallas_call(
        functools.partial(_gdn_body, Tt=Tt, C=C, HB=HB, H=H, KW=KW, NT=NT),
        grid=(nb, H // HB, NT),
        in_specs=in_specs,
        out_specs=[pl.BlockSpec((Tt, W), lambda b, h, t: (rb0 + b * NT + t, h)),
                   pl.BlockSpec((None, HB, HEAD_DIM, HEAD_DIM), lambda b, h, t: (b, h, 0, 0))],
        out_shape=[jax.ShapeDtypeStruct((M, GW), BF16),
                   jax.ShapeDtypeStruct((nb, H, HEAD_DIM, HEAD_DIM), F32)],
        scratch_shapes=[pltpu.VMEM((SUBLANES + C, W), F32)] * 3 + [pltpu.VMEM((HB, HEAD_DIM, HEAD_DIM), F32)],
        input_output_aliases={len(args) - 1: 0},
        compiler_params=_cp(("parallel", "parallel", "arbitrary")),
        name="gdn",
    )(*args)
    return o, st


CONF_HDR = 32
CONF_SUB = 32


def _conf_body(a_ref, gt_ref, hist_ref, w_ref, b_ref, lg_ref, lb_ref, obuf_hbm, o_ref, hout_ref, buf_ref,
               *, Tt, KW, NT):
    del obuf_hbm
    t = pl.program_id(1)
    HR = CONF_HDR
    base = HR - (KW - 1)

    @pl.when(t == 0)
    def _():
        buf_ref[base:HR, :] = hist_ref[...]

    buf_ref[HR:HR + Tt, :] = a_ref[...] * _sigmoid(gt_ref[...])
    sub = min(CONF_SUB, Tt)
    for r0 in range(0, Tt, sub):
        acc = None
        for j in range(KW):
            term = buf_ref[base + r0 + j:base + r0 + j + sub, :] * w_ref[j:j + 1, :]
            acc = term if acc is None else acc + term
        y = acc + b_ref[...]
        yc = y - jnp.mean(y, axis=-1, keepdims=True)
        yn = yc * lax.rsqrt(jnp.mean(yc * yc, axis=-1, keepdims=True) + EPS) * lg_ref[...] + lb_ref[...]
        o_ref[r0:r0 + sub, :] = _silu(yn).astype(o_ref.dtype)

    @pl.when(t == NT - 1)
    def _():
        hout_ref[...] = buf_ref[Tt + base:Tt + HR, :]

    if NT > 1:
        buf_ref[0:HR, :] = buf_ref[Tt:Tt + HR, :]


def _conformer(proj, o_buf, hist, dw_w, dw_b, ln_g, ln_b, l, *, lay, row0, nb, T, Tt):
    M = proj.shape[0]
    CC = dw_w.shape[2]
    KW = dw_w.shape[1]
    NT = T // Tt
    assert T % Tt == 0 and row0 % Tt == 0 and lay["GLU"] % CC == 0 and KW - 1 <= CONF_HDR
    assert NT == 1 or Tt >= CONF_HDR
    rb0 = row0 // Tt
    cb = lay["GLU"] // CC
    vec = lambda b, t: (l, 0, 0)
    in_specs = [pl.BlockSpec((Tt, CC), lambda b, t: (rb0 + b * NT + t, cb)),
                pl.BlockSpec((Tt, CC), lambda b, t: (rb0 + b * NT + t, cb + 1)),
                pl.BlockSpec((None, KW - 1, CC), lambda b, t: (b, 0, 0)),
                pl.BlockSpec((None, KW, CC), vec),
                pl.BlockSpec((None, 1, CC), vec),
                pl.BlockSpec((None, 1, CC), vec),
                pl.BlockSpec((None, 1, CC), vec),
                pl.BlockSpec(memory_space=pl.ANY)]
    args = [proj, proj, hist, dw_w, dw_b, ln_g, ln_b, o_buf]
    assert o_buf.shape == (M, CC) and o_buf.dtype == BF16
    o, hn = pl.pallas_call(
        functools.partial(_conf_body, Tt=Tt, KW=KW, NT=NT),
        grid=(nb, NT),
        in_specs=in_specs,
        out_specs=[pl.BlockSpec((Tt, CC), lambda b, t: (rb0 + b * NT + t, 0)),
                   pl.BlockSpec((None, KW - 1, CC), lambda b, t: (b, 0, 0))],
        out_shape=[jax.ShapeDtypeStruct((M, CC), BF16),
                   jax.ShapeDtypeStruct((nb, KW - 1, CC), F32)],
        scratch_shapes=[pltpu.VMEM((CONF_HDR + Tt, CC), F32)],
        input_output_aliases={len(args) - 1: 0},
        compiler_params=_cp(("parallel", "arbitrary")),
        name="conformer",
    )(*args)
    return o, hn


def _moba_prompt_body(q_ref, k_ref, v_ref, cq_ref, sq_ref, ck_ref, sk_ref, qg_ref, kg_ref, obuf_hbm,
                      o_ref, kh_ref, kbf_ref, vbf_ref, km_ref, sel_ref, m_ref, l_ref, acc_ref, *, NB):
    del obuf_hbm
    i = pl.program_id(2)
    BK = MOBA_BLOCK

    @pl.when(i == 0)
    def _():
        def blk(n, carry):
            r = pl.multiple_of(n * BK, BK)
            kr = _rope(_rms(k_ref[pl.ds(r, BK), :], kg_ref[...]), ck_ref[pl.ds(r, BK), :], sk_ref[pl.ds(r, BK), :])
            kh_ref[pl.ds(r, BK), :] = kr
            kbf_ref[pl.ds(r, BK), :] = kr.astype(BF16)
            vbf_ref[pl.ds(r, BK), :] = v_ref[pl.ds(r, BK), :].astype(BF16)
            km_ref[pl.ds(n, 1), :] = jnp.sum(kr, axis=0, keepdims=True) * (1.0 / BK)
            return carry
        lax.fori_loop(0, NB, blk, 0)

    qr = _rope(_rms(q_ref[...], qg_ref[...]), cq_ref[...], sq_ref[...])
    gate = _dot3_nt(qr, km_ref[...])
    nidx = lax.broadcasted_iota(I32, gate.shape, 1)
    past = nidx < i
    gm = jnp.where(past, gate, NEG_INF)
    rank = jnp.zeros(gate.shape, I32)
    for m in range(NB):
        gcol = gm[:, m:m + 1]
        beats = (gcol > gm) | ((gcol == gm) & (nidx > m))
        rank = rank + jnp.where(beats, 1, 0)
    sel_ref[...] = jnp.where(past & (rank < MOBA_TOPK), 1.0, 0.0)

    qs = (qr * (HEAD_DIM ** -0.5)).astype(BF16)
    m_ref[...] = jnp.full(m_ref.shape, NEG_INF, F32)
    l_ref[...] = jnp.zeros(l_ref.shape, F32)
    acc_ref[...] = jnp.zeros(acc_ref.shape, F32)

    def step(s, vblk):
        m_prev = m_ref[...]
        m_new = jnp.maximum(m_prev, jnp.max(s, axis=1, keepdims=True))
        alpha = jnp.exp(m_prev - m_new)
        p = jnp.exp(s - m_new)
        l_ref[...] = alpha * l_ref[...] + jnp.sum(p, axis=1, keepdims=True)
        acc_ref[...] = alpha * acc_ref[...] + jnp.dot(p.astype(BF16), vblk, preferred_element_type=F32)
        m_ref[...] = m_new

    for n in range(NB - 1):
        @pl.when(n < i)
        def _():
            s = _nt(qs, kbf_ref[n * BK:(n + 1) * BK, :])
            s = jnp.where(sel_ref[:, n:n + 1] > 0.5, s, NEG_INF)
            step(s, vbf_ref[n * BK:(n + 1) * BK, :])

    r = pl.multiple_of(i * BK, BK)
    s = _nt(qs, kbf_ref[pl.ds(r, BK), :])
    ri = lax.broadcasted_iota(I32, s.shape, 0)
    ci = lax.broadcasted_iota(I32, s.shape, 1)
    s = jnp.where(ci <= ri, s, NEG_INF)
    step(s, vbf_ref[pl.ds(r, BK), :])
    o_ref[...] = (acc_ref[...] / l_ref[...]).astype(o_ref.dtype)


def _moba_prompt(proj, o_buf, cos_t, sin_t, qg, kg, l, *, lay, B, S, Hm):
    M = proj.shape[0]
    BK = MOBA_BLOCK
    assert S % BK == 0
    NB = S // BK
    MW = Hm * HEAD_DIM
    hd = HEAD_DIM
    vec = lambda b, h, i: (l, 0, 0)
    o, kh = pl.pallas_call(
        functools.partial(_moba_prompt_body, NB=NB),
        grid=(B, Hm, NB),
        in_specs=[pl.BlockSpec((BK, hd), lambda b, h, i: (b * NB + i, lay["MQ"] // hd + h)),
                  pl.BlockSpec((S, hd), lambda b, h, i: (b, lay["MK"] // hd + h)),
                  pl.BlockSpec((S, hd), lambda b, h, i: (b, lay["MV"] // hd + h)),
                  pl.BlockSpec((BK, hd), lambda b, h, i: (i, 0)),
                  pl.BlockSpec((BK, hd), lambda b, h, i: (i, 0)),
                  pl.BlockSpec((S, hd), lambda b, h, i: (0, 0)),
                  pl.BlockSpec((S, hd), lambda b, h, i: (0, 0)),
                  pl.BlockSpec((None, 1, hd), vec),
                  pl.BlockSpec((None, 1, hd), vec),
                  pl.BlockSpec(memory_space=pl.ANY)],
        out_specs=[pl.BlockSpec((BK, hd), lambda b, h, i: (b * NB + i, h)),
                   pl.BlockSpec((None, S, hd), lambda b, h, i: (b, 0, h))],
        out_shape=[jax.ShapeDtypeStruct((M, MW), BF16),
                   jax.ShapeDtypeStruct((B, S, MW), F32)],
        scratch_shapes=[pltpu.VMEM((S, hd), BF16), pltpu.VMEM((S, hd), BF16),
                        pltpu.VMEM((NB, hd), F32), pltpu.VMEM((BK, NB), F32),
                        pltpu.VMEM((BK, 1), F32), pltpu.VMEM((BK, 1), F32), pltpu.VMEM((BK, hd), F32)],
        input_output_aliases={9: 0},
        compiler_params=_cp(("parallel", "parallel", "arbitrary")),
        name="moba_prompt",
    )(proj, proj, proj, cos_t, sin_t, cos_t, sin_t, qg, kg, o_buf)
    return o, kh


KM_GROUP = 8


def _kmeans_body(pt_ref, *refs, PPB, PAGE):
    ins, o_ref = refs[:-1], refs[-1]
    rows = []
    for g in range(KM_GROUP):
        s = None
        for p in range(PPB):
            part = jnp.sum(ins[g * PPB + p][...], axis=0, keepdims=True)
            s = part if s is None else s + part
        rows.append(s * (1.0 / (PPB * PAGE)))
    o_ref[...] = jnp.concatenate(rows, axis=0)


def _cache_block_means(cache_k4, pt_flat, *, DB, n_pages):
    L, _, PAGE, MW = cache_k4.shape
    PPB = MOBA_BLOCK // PAGE
    NBK = n_pages // PPB
    assert MOBA_BLOCK % PAGE == 0 and n_pages % PPB == 0 and NBK % KM_GROUP == 0
    PG = KM_GROUP * PPB

    def page_spec(p):
        return pl.BlockSpec((None, None, PAGE, MW),
                            lambda l, b, g, pt: (l, pt[b * n_pages + g * PG + p], 0, 0))

    return pl.pallas_call(
        functools.partial(_kmeans_body, PPB=PPB, PAGE=PAGE),
        grid_spec=pltpu.PrefetchScalarGridSpec(
            num_scalar_prefetch=1,
            grid=(L, DB, NBK // KM_GROUP),
            in_specs=[page_spec(p) for p in range(PG)],
            out_specs=pl.BlockSpec((None, None, KM_GROUP, MW), lambda l, b, g, pt: (l, b, g, 0))),
        out_shape=jax.ShapeDtypeStruct((L, DB, NBK, MW), F32),
        compiler_params=_cp(("parallel", "parallel", "arbitrary")),
        name="cache_block_means",
    )(pt_flat, *([cache_k4] * PG))


def _moba_gate_body(qkv_ref, cos_ref, sin_ref, qg_ref, kg_ref, km_ref, q_ref, k_ref, sel_ref, *, Hm, NBK):
    MW = Hm * HEAD_DIM
    T = qkv_ref.shape[0]
    lane_o = lax.broadcasted_iota(I32, (T, LANES), 1)
    lane_g = lax.broadcasted_iota(I32, (T, NBK), 1)
    sel = jnp.zeros((T, LANES), I32)
    for h in range(Hm):
        c0 = h * HEAD_DIM
        qr = _rope(_rms(qkv_ref[:, c0:c0 + HEAD_DIM], qg_ref[...]), cos_ref[...], sin_ref[...])
        kr = _rope(_rms(qkv_ref[:, MW + c0:MW + c0 + HEAD_DIM], kg_ref[...]), cos_ref[...], sin_ref[...])
        q_ref[:, c0:c0 + HEAD_DIM] = qr
        k_ref[:, c0:c0 + HEAD_DIM] = kr
        gate = _dot3_nt(qr, km_ref[:, c0:c0 + HEAD_DIM])
        for j in range(MOBA_TOPK):
            mx = jnp.max(gate, axis=1, keepdims=True)
            idx = jnp.min(jnp.where(gate == mx, lane_g, NBK), axis=1, keepdims=True)
            sel = jnp.where(lane_o == h * MOBA_TOPK + j, idx, sel)
            gate = jnp.where(lane_g == idx, -jnp.inf, gate)
    sel_ref[...] = sel


def _moba_gate(proj, cos_s, sin_s, qg, kg, km, l, *, lay, row0, DB, T, Hm):
    MW = Hm * HEAD_DIM
    NBK = km.shape[2]
    assert row0 % T == 0 and lay["MQ"] == 0 and lay["MK"] == MW and Hm * MOBA_TOPK <= LANES and NBK >= MOBA_TOPK
    vec = lambda b: (l, 0, 0)
    return pl.pallas_call(
        functools.partial(_moba_gate_body, Hm=Hm, NBK=NBK),
        grid=(DB,),
        in_specs=[pl.BlockSpec((T, 3 * MW), lambda b: (row0 // T + b, 0)),
                  pl.BlockSpec((T, HEAD_DIM), lambda b: (0, 0)),
                  pl.BlockSpec((T, HEAD_DIM), lambda b: (0, 0)),
                  pl.BlockSpec((None, 1, HEAD_DIM), vec),
                  pl.BlockSpec((None, 1, HEAD_DIM), vec),
                  pl.BlockSpec((None, None, NBK, MW), lambda b: (l, b, 0, 0))],
        out_specs=[pl.BlockSpec((T, MW), lambda b: (b, 0)),
                   pl.BlockSpec((T, MW), lambda b: (b, 0)),
                   pl.BlockSpec((None, T, LANES), lambda b: (b, 0, 0))],
        out_shape=[jax.ShapeDtypeStruct((DB * T, MW), F32),
                   jax.ShapeDtypeStruct((DB * T, MW), F32),
                   jax.ShapeDtypeStruct((DB, T, LANES), I32)],
        compiler_params=_cp(("parallel",)),
        name="moba_gate",
    )(proj, cos_s, sin_s, qg, kg, km)


def _moba_sample_body(pt_ref, sel_ref, q_ref, k_ref, v_ref, ck_hbm, cv_hbm, obuf_hbm, o_ref,
                      kbuf_ref, vbuf_ref, sem_ref, *, l, T, Hm, n_pages, PPB, PAGE):
    del obuf_hbm
    b = pl.program_id(0)
    h = pl.program_id(1)
    BK = MOBA_BLOCK
    NS = T * MOBA_TOPK
    c0 = pl.multiple_of(h * HEAD_DIM, HEAD_DIM)

    def copies(slot, p):
        t, j = divmod(slot, MOBA_TOPK)
        blk = sel_ref[((b * T + t) * Hm + h) * MOBA_TOPK + j]
        page = pt_ref[b * n_pages + blk * PPB + p]
        dst = pl.ds(slot * BK + p * PAGE, PAGE)
        return (pltpu.make_async_copy(ck_hbm.at[l, page, :, pl.ds(c0, HEAD_DIM)], kbuf_ref.at[dst, :], sem_ref.at[0]),
                pltpu.make_async_copy(cv_hbm.at[l, page, :, pl.ds(c0, HEAD_DIM)], vbuf_ref.at[dst, :], sem_ref.at[1]))

    for slot in range(NS):
        for p in range(PPB):
            for c in copies(slot, p):
                c.start()
    for slot in range(NS):
        for p in range(PPB):
            for c in copies(slot, p):
                c.wait()

    TP = 2 * SUBLANES
    zpad = jnp.zeros((TP - T, HEAD_DIM), F32)
    qs = jnp.concatenate([q_ref[...] * (HEAD_DIM ** -0.5), zpad], axis=0).astype(BF16)
    kn = jnp.concatenate([k_ref[...], zpad], axis=0).astype(BF16)
    vn = jnp.concatenate([v_ref[...], zpad], axis=0).astype(BF16)
    s_p = _nt(qs, kbuf_ref[...].astype(BF16))
    ri = lax.broadcasted_iota(I32, s_p.shape, 0)
    ci = lax.broadcasted_iota(I32, s_p.shape, 1)
    span = MOBA_TOPK * BK
    s_p = jnp.where((ci >= ri * span) & (ci < (ri + 1) * span), s_p, NEG_INF)
    s_o = _nt(qs, kn)
    ro = lax.broadcasted_iota(I32, s_o.shape, 0)
    co = lax.broadcasted_iota(I32, s_o.shape, 1)
    s_o = jnp.where((co <= ro) & (co < T), s_o, NEG_INF)
    m = jnp.maximum(jnp.max(s_p, axis=1, keepdims=True), jnp.max(s_o, axis=1, keepdims=True))
    p_p = jnp.exp(s_p - m)
    p_o = jnp.exp(s_o - m)
    den = jnp.sum(p_p, axis=1, keepdims=True) + jnp.sum(p_o, axis=1, keepdims=True)
    o = (jnp.dot(p_p.astype(BF16), vbuf_ref[...].astype(BF16), preferred_element_type=F32)
         + jnp.dot(p_o.astype(BF16), vn, preferred_element_type=F32)) / den
    o_ref[...] = o[:T].astype(o_ref.dtype)


def _moba_sample(proj, o_buf, q_rot, k_new, sel_flat, pt_flat, cache_k4, cache_v4, l, *, lay, row0, DB, T, Hm, n_pages):
    PAGE = cache_k4.shape[2]
    PPB = MOBA_BLOCK // PAGE
    NS = T * MOBA_TOPK
    hd = HEAD_DIM
    assert T <= 2 * SUBLANES and row0 % T == 0
    grid_spec = pltpu.PrefetchScalarGridSpec(
        num_scalar_prefetch=2,
        grid=(DB, Hm),
        in_specs=[pl.BlockSpec((T, hd), lambda b, h, pt, sl: (b, h)),
                  pl.BlockSpec((T, hd), lambda b, h, pt, sl: (b, h)),
                  pl.BlockSpec((T, hd), lambda b, h, pt, sl: (row0 // T + b, lay["MV"] // hd + h)),
                  pl.BlockSpec(memory_space=pl.ANY),
                  pl.BlockSpec(memory_space=pl.ANY),
                  pl.BlockSpec(memory_space=pl.ANY)],
        out_specs=pl.BlockSpec((T, hd), lambda b, h, pt, sl: (row0 // T + b, h)),
        scratch_shapes=[pltpu.VMEM((NS * MOBA_BLOCK, hd), F32), pltpu.VMEM((NS * MOBA_BLOCK, hd), F32),
                        pltpu.SemaphoreType.DMA((2,))])
    return pl.pallas_call(
        functools.partial(_moba_sample_body, l=l, T=T, Hm=Hm, n_pages=n_pages, PPB=PPB, PAGE=PAGE),
        grid_spec=grid_spec,
        out_shape=jax.ShapeDtypeStruct(o_buf.shape, o_buf.dtype),
        input_output_aliases={7: 0},
        compiler_params=_cp(("arbitrary", "arbitrary")),
        name="moba_sample",
    )(pt_flat, sel_flat, q_rot, k_new, proj, cache_k4, cache_v4, o_buf)


def _layout(GW, MW, CC):
    def up(x, m):
        return (x + m - 1) // m * m
    lay = {"MQ": 0, "MK": MW, "MV": 2 * MW, "GQ": 3 * MW, "GK": 3 * MW + GW, "GV": 3 * MW + 2 * GW,
           "GZ": 3 * MW + 3 * GW}
    lay["BA"] = up(lay["GZ"] + GW, BA_WIDTH)
    lay["GLU"] = up(lay["BA"] + BA_WIDTH, CC)
    lay["NP"] = up(lay["GLU"] + 2 * CC, 4 * LANES)
    return lay


def _arrange_w_in(w_in, lay, GW, MW, CC, Hg):
    L, D, _ = w_in.shape
    o = 0
    qkv_a = w_in[..., o:o + 3 * GW]; o += 3 * GW
    z_a = w_in[..., o:o + GW]; o += GW
    b_a = w_in[..., o:o + Hg]; o += Hg
    a_a = w_in[..., o:o + Hg]; o += Hg
    glu = w_in[..., o:o + 2 * CC]; o += 2 * CC
    qkv_c = w_in[..., o:o + 3 * MW]; o += 3 * MW
    assert o == w_in.shape[-1]
    z = lambda n: jnp.zeros((L, D, n), w_in.dtype)
    parts = [qkv_c, qkv_a, z_a, z(lay["BA"] - lay["GZ"] - GW), b_a, z(LANES - Hg), a_a, z(LANES - Hg),
             z(lay["GLU"] - lay["BA"] - BA_WIDTH), glu, z(lay["NP"] - lay["GLU"] - 2 * CC)]
    return jnp.concatenate(parts, axis=-1).astype(BF16)


def _rope_tables(pos):
    half = HEAD_DIM // 2
    inv_freq = ROPE_THETA ** (-jnp.arange(half, dtype=F32) / half)
    ang = pos.astype(F32)[:, None] * inv_freq[None, :]
    cos, sin = jnp.cos(ang), jnp.sin(ang)
    return jnp.concatenate([cos, cos], axis=-1), jnp.concatenate([-sin, sin], axis=-1)


def _pad_lanes(a):
    L, H = a.shape
    return jnp.pad(a, ((0, 0), (0, LANES - H))).reshape(L, 1, LANES)


def kernel(x_prompt, x_sample, cache_k, cache_v, state_gdn, state_gdn_conv, state_conv, page_table, norm_mix, w_in, gdn_conv_w, gdn_a_log, gdn_dt_bias, gdn_out_norm, conv_dw_w, conv_dw_b, conv_ln_g, conv_ln_b, moba_q_norm, moba_k_norm, w_out, norm_ffn, w_gate, w_up, w_down):
    B, S, D = x_prompt.shape
    DB, T, _ = x_sample.shape
    L = w_in.shape[0]
    Hg = gdn_a_log.shape[1]
    Hm = cache_k.shape[3]
    PAGE = cache_k.shape[2]
    n_pool = cache_k.shape[1]
    n_pages = page_table.shape[1]
    CC = conv_dw_b.shape[1]
    KWG = gdn_conv_w.shape[1]
    KWC = conv_dw_w.shape[1]
    FH = w_gate.shape[2]
    GW, MW = Hg * HEAD_DIM, Hm * HEAD_DIM
    past_len = n_pages * PAGE
    assert cache_k.shape[4] == HEAD_DIM and past_len % MOBA_BLOCK == 0 and GW + CC + MW == D
    assert (past_len + T - 1) // MOBA_BLOCK == past_len // MOBA_BLOCK
    lay = _layout(GW, MW, CC)
    NP = lay["NP"]
    MP, MS = B * S, DB * T
    M = MP + MS

    w_in_r = _arrange_w_in(w_in, lay, GW, MW, CC, Hg)
    wo = w_out.astype(BF16)
    wo_a, wo_b, wo_c = wo[:, :GW], wo[:, GW:GW + CC], wo[:, GW + CC:]
    wg, wu, wd = w_gate.astype(BF16), w_up.astype(BF16), w_down.astype(BF16)
    vec3 = lambda a: a.reshape(L, 1, a.shape[-1])
    norm_mix3, norm_ffn3 = vec3(norm_mix), vec3(norm_ffn)
    og3, qg3, kg3 = vec3(gdn_out_norm), vec3(moba_q_norm), vec3(moba_k_norm)
    dwb3, lng3, lnb3 = vec3(conv_dw_b), vec3(conv_ln_g), vec3(conv_ln_b)
    alog3, dtb3 = _pad_lanes(gdn_a_log), _pad_lanes(gdn_dt_bias)
    cos_p, sin_p = _rope_tables(jnp.arange(S, dtype=I32))
    cos_s, sin_s = _rope_tables(past_len + jnp.arange(T, dtype=I32))
    cache_k4 = cache_k.reshape(L, n_pool, PAGE, MW)
    cache_v4 = cache_v.reshape(L, n_pool, PAGE, MW)
    pt_flat = page_table.reshape(-1).astype(I32)
    gdn_hist0 = jnp.zeros((B, KWG - 1, 3 * GW), F32)
    gdn_state0 = jnp.zeros((B, Hg, HEAD_DIM, HEAD_DIM), F32)
    conv_hist0 = jnp.zeros((B, KWC - 1, CC), F32)

    tm_in = _largest_divisor(M, 1400, SUBLANES)
    tn_in = _largest_divisor(NP, 512, LANES)
    tm_ffn = _largest_divisor(M, 700, SUBLANES)
    th_ffn = _largest_divisor(FH, 512, LANES)
    gdn_c = MOBA_BLOCK if S % MOBA_BLOCK == 0 else _largest_divisor(S, 256, SUBLANES)
    conf_t = _largest_divisor(S, 256, CONF_HDR)

    km_all = _cache_block_means(cache_k4, pt_flat, DB=DB, n_pages=n_pages)

    x = jnp.concatenate([x_prompt.reshape(MP, D), x_sample.reshape(MS, D)], axis=0)
    o_a = jnp.zeros((M, GW), BF16)
    o_b = jnp.zeros((M, CC), BF16)
    o_c = jnp.zeros((M, MW), BF16)
    outs = [[] for _ in range(10)]
    for l in range(L):
        proj = _inproj(x, norm_mix3, w_in_r, l, tm_in, tn_in)
        o_a, st_p = _gdn(proj, o_a, gdn_hist0, gdn_state0, gdn_conv_w, alog3, dtb3, og3, l,
                         lay=lay, row0=0, nb=B, T=S, Tt=gdn_c, C=gdn_c, HB=Hg)
        o_a, st_s = _gdn(proj, o_a, state_gdn_conv[l], state_gdn[l], gdn_conv_w, alog3, dtb3, og3, l,
                         lay=lay, row0=MP, nb=DB, T=T, Tt=T, C=LANES, HB=Hg)
        o_b, ch_p = _conformer(proj, o_b, conv_hist0, conv_dw_w, dwb3, lng3, lnb3, l,
                               lay=lay, row0=0, nb=B, T=S, Tt=conf_t)
        o_b, ch_s = _conformer(proj, o_b, state_conv[l], conv_dw_w, dwb3, lng3, lnb3, l,
                               lay=lay, row0=MP, nb=DB, T=T, Tt=T)
        o_c, kh_p = _moba_prompt(proj, o_c, cos_p, sin_p, qg3, kg3, l, lay=lay, B=B, S=S, Hm=Hm)
        q_rot, k_new, sel = _moba_gate(proj, cos_s, sin_s, qg3, kg3, km_all, l,
                                       lay=lay, row0=MP, DB=DB, T=T, Hm=Hm)
        sel_flat = sel[:, :, :Hm * MOBA_TOPK].reshape(-1)
        o_c = _moba_sample(proj, o_c, q_rot, k_new, sel_flat, pt_flat, cache_k4, cache_v4, l,
                           lay=lay, row0=MP, DB=DB, T=T, Hm=Hm, n_pages=n_pages)
        x = _outproj_ffn(x, o_a, o_b, o_c, wo_a, wo_b, wo_c, norm_ffn3, wg, wu, wd, l, tm_ffn, th_ffn)

        vcols = proj[:, lay["MV"]:lay["MV"] + MW]
        gcols = proj[:, lay["GQ"]:lay["GQ"] + 3 * GW]
        outs[0].append(kh_p.reshape(B, S, Hm, HEAD_DIM))
        outs[1].append(vcols[:MP].reshape(B, S, Hm, HEAD_DIM))
        outs[2].append(k_new.reshape(DB, T, Hm, HEAD_DIM))
        outs[3].append(vcols[MP:].reshape(DB, T, Hm, HEAD_DIM))
        outs[4].append(st_p)
        outs[5].append(st_s)
        outs[6].append(gcols[:MP].reshape(B, S, 3 * GW)[:, S - (KWG - 1):])
        hist_ext = jnp.concatenate([state_gdn_conv[l], gcols[MP:].reshape(DB, T, 3 * GW)], axis=1)
        outs[7].append(hist_ext[:, -(KWG - 1):])
        outs[8].append(ch_p)
        outs[9].append(ch_s)

    y_prompt = x[:MP].reshape(B, S, D)
    y_sample = x[MP:].reshape(DB, T, D)
    return (y_prompt, y_sample) + tuple(jnp.stack(o) for o in outs)
```

```python
import functools

import jax
import jax.numpy as jnp
from jax import lax
from jax.experimental import pallas as pl
from jax.experimental.pallas import tpu as pltpu

F32 = jnp.float32
BF16 = jnp.bfloat16
I32 = jnp.int32

EPS = 1e-6
NEG_INF = -1e30
HEAD_DIM = 128
MOBA_BLOCK = 256
MOBA_TOPK = 3
ROPE_THETA = 10000.0
LANES = 128
SUBLANES = 8
BA_WIDTH = 2 * LANES
VMEM_LIMIT = 60 * 1024 * 1024


def _cp(sem, vmem=VMEM_LIMIT):
    return pltpu.CompilerParams(dimension_semantics=sem, vmem_limit_bytes=vmem)


def _bdot(a, b):
    return jnp.dot(a.astype(BF16), b.astype(BF16), preferred_element_type=F32)


def _nt(a, b):
    return lax.dot_general(a, b, (((1,), (1,)), ((), ())), preferred_element_type=F32)


def _bdot_nt(a, b):
    return _nt(a.astype(BF16), b.astype(BF16))


def _split_bf16(a):
    hi = a.astype(BF16)
    lo = (a - hi.astype(F32)).astype(BF16)
    return hi, lo


def _dot3_nt(a, b):
    ah, al = _split_bf16(a)
    bh, bl = _split_bf16(b)
    return _nt(ah, bh) + (_nt(ah, bl) + _nt(al, bh))


def _sigmoid(x):
    return 1.0 / (1.0 + jnp.exp(-x))


def _silu(x):
    return x * _sigmoid(x)


def _softplus(x):
    return jnp.maximum(x, 0.0) + jnp.log(1.0 + jnp.exp(-jnp.abs(x)))


def _rms(x, g):
    return x * lax.rsqrt(jnp.mean(x * x, axis=-1, keepdims=True) + EPS) * g


def _rope(x, cos_full, sin_signed):
    return x * cos_full + pltpu.roll(x, HEAD_DIM // 2, 1) * sin_signed


def _largest_divisor(n, cap, mult):
    best = None
    for d in range(mult, min(n, cap) + 1, mult):
        if n % d == 0:
            best = d
    assert best is not None, (n, cap, mult)
    return best


def _inproj_body(x_ref, g_ref, w_ref, o_ref, xn_ref):
    @pl.when(pl.program_id(1) == 0)
    def _():
        xn_ref[...] = _rms(x_ref[...], g_ref[...]).astype(BF16)

    o_ref[...] = jnp.dot(xn_ref[...], w_ref[...], preferred_element_type=F32)


def _inproj(x, g, w, l, tm, tn):
    M, D = x.shape
    NP = w.shape[-1]
    return pl.pallas_call(
        _inproj_body,
        grid=(M // tm, NP // tn),
        in_specs=[pl.BlockSpec((tm, D), lambda i, j: (i, 0)),
                  pl.BlockSpec((None, 1, D), lambda i, j: (l, 0, 0)),
                  pl.BlockSpec((None, D, tn), lambda i, j: (l, 0, j))],
        out_specs=pl.BlockSpec((tm, tn), lambda i, j: (i, j)),
        out_shape=jax.ShapeDtypeStruct((M, NP), F32),
        scratch_shapes=[pltpu.VMEM((tm, D), BF16)],
        compiler_params=_cp(("parallel", "arbitrary")),
        name="inproj",
    )(x, g, w)


def _ffn_body(x_ref, oa_ref, ob_ref, oc_ref, wa_ref, wb_ref, wc_ref, g_ref, wg_ref, wu_ref, wd_ref,
              y_ref, hn_ref):
    @pl.when(pl.program_id(1) == 0)
    def _():
        h = (x_ref[...]
             + jnp.dot(oa_ref[...], wa_ref[...], preferred_element_type=F32)
             + jnp.dot(ob_ref[...], wb_ref[...], preferred_element_type=F32)
             + jnp.dot(oc_ref[...], wc_ref[...], preferred_element_type=F32))
        y_ref[...] = h
        hn_ref[...] = _rms(h, g_ref[...]).astype(BF16)

    hn = hn_ref[...]
    a = jnp.dot(hn, wg_ref[...], preferred_element_type=F32)
    b = jnp.dot(hn, wu_ref[...], preferred_element_type=F32)
    t = (_silu(a) * b).astype(BF16)
    y_ref[...] += jnp.dot(t, wd_ref[...], preferred_element_type=F32)


def _outproj_ffn(x, oa, ob, oc, wa, wb, wc, g, wg, wu, wd, l, tm, th):
    M, D = x.shape
    FH = wg.shape[-1]
    GW, CC, MW = oa.shape[1], ob.shape[1], oc.shape[1]
    row = lambda i, j: (i, 0)
    return pl.pallas_call(
        _ffn_body,
        grid=(M // tm, FH // th),
        in_specs=[pl.BlockSpec((tm, D), row),
                  pl.BlockSpec((tm, GW), row),
                  pl.BlockSpec((tm, CC), row),
                  pl.BlockSpec((tm, MW), row),
                  pl.BlockSpec((None, GW, D), lambda i, j: (l, 0, 0), pipeline_mode=pl.Buffered(1)),
                  pl.BlockSpec((None, CC, D), lambda i, j: (l, 0, 0), pipeline_mode=pl.Buffered(1)),
                  pl.BlockSpec((None, MW, D), lambda i, j: (l, 0, 0), pipeline_mode=pl.Buffered(1)),
                  pl.BlockSpec((None, 1, D), lambda i, j: (l, 0, 0)),
                  pl.BlockSpec((None, D, th), lambda i, j: (l, 0, j)),
                  pl.BlockSpec((None, D, th), lambda i, j: (l, 0, j)),
                  pl.BlockSpec((None, th, D), lambda i, j: (l, j, 0))],
        out_specs=pl.BlockSpec((tm, D), row),
        out_shape=jax.ShapeDtypeStruct((M, D), F32),
        scratch_shapes=[pltpu.VMEM((tm, D), BF16)],
        compiler_params=_cp(("parallel", "arbitrary")),
        name="outproj_ffn",
    )(x, oa, ob, oc, wa, wb, wc, g, wg, wu, wd)


def _gdn_chunks(qs, ks, vs, betas, gs, Ss):
    nh = len(qs)
    hs = range(nh)
    C, dk = qs[0].shape
    assert C & (C - 1) == 0
    ri = lax.broadcasted_iota(I32, (C, C), 0)
    ci = lax.broadcasted_iota(I32, (C, C), 1)
    incl = ri >= ci
    eye = ri == ci
    xor = ri ^ ci
    g_row = [jnp.sum(jnp.where(eye, gs[h], 0.0), axis=0, keepdims=True) for h in hs]
    gc_row = [jnp.sum(jnp.where(ri <= ci, gs[h], 0.0), axis=0, keepdims=True) for h in hs]
    gc_col = [jnp.sum(jnp.where(incl, g_row[h], 0.0), axis=1, keepdims=True) for h in hs]
    decay = [jnp.where(incl, jnp.exp(jnp.where(incl, gc_col[h] - gc_row[h], 0.0)), 0.0) for h in hs]
    kbf = [ks[h].astype(BF16) for h in hs]
    kb = [ks[h] * betas[h] for h in hs]
    a = [jnp.where(ri > ci, _nt(kb[h].astype(BF16), kbf[h]) * decay[h], 0.0).astype(BF16) for h in hs]
    p = [jnp.where(eye, 1.0, 0.0) - jnp.where(xor == 1, a[h].astype(F32), 0.0) for h in hs]
    for lvl in range(1, C.bit_length() - 1):
        off = jnp.where((xor >> lvl) == 1, 1.0, 0.0).astype(BF16)
        pb = [p[h].astype(BF16) for h in hs]
        t1 = [jnp.dot(a[h] * off, pb[h], preferred_element_type=F32).astype(BF16) for h in hs]
        p = [p[h] - jnp.dot(pb[h], t1[h], preferred_element_type=F32) for h in hs]
    pb = [p[h].astype(BF16) for h in hs]
    egc = [jnp.exp(gc_col[h]) for h in hs]
    u = [jnp.dot(pb[h], (vs[h] * betas[h]).astype(BF16), preferred_element_type=F32) for h in hs]
    w = [jnp.dot(pb[h], (kb[h] * egc[h]).astype(BF16), preferred_element_type=F32) for h in hs]
    sb = [Ss[h].astype(BF16) for h in hs]
    v_new = [u[h] - jnp.dot(w[h].astype(BF16), sb[h], preferred_element_type=F32) for h in hs]
    vnb = [v_new[h].astype(BF16) for h in hs]
    qsc = [qs[h] * (dk ** -0.5) for h in hs]
    intra = [jnp.where(incl, _nt(qsc[h].astype(BF16), kbf[h]) * decay[h], 0.0).astype(BF16) for h in hs]
    o = [jnp.dot((qsc[h] * egc[h]).astype(BF16), sb[h], preferred_element_type=F32)
         + jnp.dot(intra[h], vnb[h], preferred_element_type=F32) for h in hs]
    g_last = [gc_col[h][C - 1:C, :] for h in hs]
    kdec_t = [(ks[h] * jnp.exp(g_last[h] - gc_col[h])).T.astype(BF16) for h in hs]
    s_new = [Ss[h] * jnp.exp(g_last[h]) + jnp.dot(kdec_t[h], vnb[h], preferred_element_type=F32) for h in hs]
    return o, s_new


def _gdn_body(q_ref, k_ref, v_ref, z_ref, ba_ref, hq_ref, hk_ref, hv_ref, s0_ref,
              wq_ref, wk_ref, wv_ref, alog_ref, dtb_ref, og_ref, obuf_hbm,
              o_ref, sout_ref, xq_ref, xk_ref, xv_ref, s_ref, *, Tt, C, HB, H, KW, NT):
    del obuf_hbm
    hb = pl.program_id(1)
    t = pl.program_id(2)
    HR = SUBLANES
    xrefs = (xq_ref, xk_ref, xv_ref)

    @pl.when(t == 0)
    def _():
        for xr, hr in zip(xrefs, (hq_ref, hk_ref, hv_ref)):
            xr[HR - (KW - 1):HR, :] = hr[...]
        s_ref[...] = s0_ref[...]

    for xr, r in zip(xrefs, (q_ref, k_ref, v_ref)):
        xr[HR:HR + Tt, :] = r[...]
        if C > Tt:
            xr[HR + Tt:HR + C, :] = jnp.zeros((C - Tt, xr.shape[1]), F32)

    def conv(xr, w_ref, s):
        acc = None
        for j in range(KW):
            term = xr[HR - (KW - 1) + j:HR - (KW - 1) + j + C, s * LANES:(s + 1) * LANES] \
                * w_ref[j:j + 1, s * LANES:(s + 1) * LANES]
            acc = term if acc is None else acc + term
        return acc

    ba = ba_ref[...]
    if C > Tt:
        ba = jnp.concatenate([ba, jnp.zeros((C - Tt, ba.shape[1]), F32)], axis=0)
    beta_all = _sigmoid(ba[:, :LANES])
    g_all = -jnp.exp(alog_ref[...]) * _softplus(ba[:, LANES:] + dtb_ref[...])
    lane = lax.broadcasted_iota(I32, (1, LANES), 1)
    rvalid = lax.broadcasted_iota(I32, (C, 1), 0) < Tt

    qs, ks, vs, betas, gs = [], [], [], [], []
    for s in range(HB):
        hsel = lane == hb * HB + s
        beta = jnp.sum(jnp.where(hsel, beta_all, 0.0), axis=1, keepdims=True)
        g = jnp.sum(jnp.where(hsel, g_all, 0.0), axis=1, keepdims=True)
        q = _silu(conv(xq_ref, wq_ref, s))
        k = _silu(conv(xk_ref, wk_ref, s))
        v = _silu(conv(xv_ref, wv_ref, s))
        q = q * lax.rsqrt(jnp.sum(q * q, axis=-1, keepdims=True) + EPS)
        k = k * lax.rsqrt(jnp.sum(k * k, axis=-1, keepdims=True) + EPS)
        if C > Tt:
            q = jnp.where(rvalid, q, 0.0)
            k = jnp.where(rvalid, k, 0.0)
            v = jnp.where(rvalid, v, 0.0)
            beta = jnp.where(rvalid, beta, 0.0)
            g = jnp.where(rvalid, g, 0.0)
        for lst, val in zip((qs, ks, vs, betas, gs), (q, k, v, beta, g)):
            lst.append(val)
    os_, s_new = _gdn_chunks(qs, ks, vs, betas, gs, [s_ref[s] for s in range(HB)])
    for s in range(HB):
        s_ref[s] = s_new[s]
        zz = z_ref[:, s * LANES:(s + 1) * LANES]
        o_ref[:, s * LANES:(s + 1) * LANES] = (_rms(os_[s][:Tt], og_ref[...]) * _silu(zz)).astype(o_ref.dtype)

    if NT > 1:
        for xr in xrefs:
            xr[0:HR, :] = xr[C:C + HR, :]

    @pl.when(t == NT - 1)
    def _():
        sout_ref[...] = s_ref[...]


def _gdn(proj, o_buf, hist, s0, conv_w, alog, dtb, og, l, *, lay, row0, nb, T, Tt, C, HB):
    M = proj.shape[0]
    H = s0.shape[1]
    GW = H * HEAD_DIM
    KW = conv_w.shape[1]
    NT = T // Tt
    W = HB * LANES
    assert T % Tt == 0 and row0 % Tt == 0 and H % HB == 0 and (NT == 1 or Tt == C)
    rb0 = row0 // Tt

    def col(off):
        assert off % W == 0
        return lambda b, h, t: (rb0 + b * NT + t, off // W + h)

    def hcol(off):
        return lambda b, h, t: (b, 0, off // W + h)

    def wcol(off):
        return lambda b, h, t: (l, 0, off // W + h)

    vec = lambda b, h, t: (l, 0, 0)
    in_specs = [pl.BlockSpec((Tt, W), col(lay["GQ"])),
                pl.BlockSpec((Tt, W), col(lay["GK"])),
                pl.BlockSpec((Tt, W), col(lay["GV"])),
                pl.BlockSpec((Tt, W), col(lay["GZ"])),
                pl.BlockSpec((Tt, BA_WIDTH), lambda b, h, t: (rb0 + b * NT + t, lay["BA"] // BA_WIDTH)),
                pl.BlockSpec((None, KW - 1, W), hcol(0)),
                pl.BlockSpec((None, KW - 1, W), hcol(GW)),
                pl.BlockSpec((None, KW - 1, W), hcol(2 * GW)),
                pl.BlockSpec((None, HB, HEAD_DIM, HEAD_DIM), lambda b, h, t: (b, h, 0, 0)),
                pl.BlockSpec((None, KW, W), wcol(0)),
                pl.BlockSpec((None, KW, W), wcol(GW)),
                pl.BlockSpec((None, KW, W), wcol(2 * GW)),
                pl.BlockSpec((None, 1, LANES), vec),
                pl.BlockSpec((None, 1, LANES), vec),
                pl.BlockSpec((None, 1, HEAD_DIM), vec),
                pl.BlockSpec(memory_space=pl.ANY)]
    args = [proj, proj, proj, proj, proj, hist, hist, hist, s0, conv_w, conv_w, conv_w, alog, dtb, og, o_buf]
    assert o_buf.shape == (M, GW) and o_buf.dtype == BF16
    o, st = pl.pallas_call(
        functools.partial(_gdn_body, Tt=Tt, C=C, HB=HB, H=H, KW=KW, NT=NT),
        grid=(nb, H // HB, NT),
        in_specs=in_specs,
        out_specs=[pl.BlockSpec((Tt, W), lambda b, h, t: (rb0 + b * NT + t, h)),
                   pl.BlockSpec((None, HB, HEAD_DIM, HEAD_DIM), lambda b, h, t: (b, h, 0, 0))],
        out_shape=[jax.ShapeDtypeStruct((M, GW), BF16),
                   jax.ShapeDtypeStruct((nb, H, HEAD_DIM, HEAD_DIM), F32)],
        scratch_shapes=[pltpu.VMEM((SUBLANES + C, W), F32)] * 3 + [pltpu.VMEM((HB, HEAD_DIM, HEAD_DIM), F32)],
        input_output_aliases={len(args) - 1: 0},
        compiler_params=_cp(("parallel", "parallel", "arbitrary")),
        name="gdn",
    )(*args)
    return o, st


CONF_HDR = 32
CONF_SUB = 32


def _conf_body(a_ref, gt_ref, hist_ref, w_ref, b_ref, lg_ref, lb_ref, obuf_hbm, o_ref, hout_ref, buf_ref,
               *, Tt, KW, NT):
    del obuf_hbm
    t = pl.program_id(1)
    HR = CONF_HDR
    base = HR - (KW - 1)

    @pl.when(t == 0)
    def _():
        buf_ref[base:HR, :] = hist_ref[...]

    buf_ref[HR:HR + Tt, :] = a_ref[...] * _sigmoid(gt_ref[...])
    sub = min(CONF_SUB, Tt)
    for r0 in range(0, Tt, sub):
        acc = None
        for j in range(KW):
            term = buf_ref[base + r0 + j:base + r0 + j + sub, :] * w_ref[j:j + 1, :]
            acc = term if acc is None else acc + term
        y = acc + b_ref[...]
        yc = y - jnp.mean(y, axis=-1, keepdims=True)
        yn = yc * lax.rsqrt(jnp.mean(yc * yc, axis=-1, keepdims=True) + EPS) * lg_ref[...] + lb_ref[...]
        o_ref[r0:r0 + sub, :] = _silu(yn).astype(o_ref.dtype)

    @pl.when(t == NT - 1)
    def _():
        hout_ref[...] = buf_ref[Tt + base:Tt + HR, :]

    if NT > 1:
        buf_ref[0:HR, :] = buf_ref[Tt:Tt + HR, :]


def _conformer(proj, o_buf, hist, dw_w, dw_b, ln_g, ln_b, l, *, lay, row0, nb, T, Tt):
    M = proj.shape[0]
    CC = dw_w.shape[2]
    KW = dw_w.shape[1]
    NT = T // Tt
    assert T % Tt == 0 and row0 % Tt == 0 and lay["GLU"] % CC == 0 and KW - 1 <= CONF_HDR
    assert NT == 1 or Tt >= CONF_HDR
    rb0 = row0 // Tt
    cb = lay["GLU"] // CC
    vec = lambda b, t: (l, 0, 0)
    in_specs = [pl.BlockSpec((Tt, CC), lambda b, t: (rb0 + b * NT + t, cb)),
                pl.BlockSpec((Tt, CC), lambda b, t: (rb0 + b * NT + t, cb + 1)),
                pl.BlockSpec((None, KW - 1, CC), lambda b, t: (b, 0, 0)),
                pl.BlockSpec((None, KW, CC), vec),
                pl.BlockSpec((None, 1, CC), vec),
                pl.BlockSpec((None, 1, CC), vec),
                pl.BlockSpec((None, 1, CC), vec),
                pl.BlockSpec(memory_space=pl.ANY)]
    args = [proj, proj, hist, dw_w, dw_b, ln_g, ln_b, o_buf]
    assert o_buf.shape == (M, CC) and o_buf.dtype == BF16
    o, hn = pl.pallas_call(
        functools.partial(_conf_body, Tt=Tt, KW=KW, NT=NT),
        grid=(nb, NT),
        in_specs=in_specs,
        out_specs=[pl.BlockSpec((Tt, CC), lambda b, t: (rb0 + b * NT + t, 0)),
                   pl.BlockSpec((None, KW - 1, CC), lambda b, t: (b, 0, 0))],
        out_shape=[jax.ShapeDtypeStruct((M, CC), BF16),
                   jax.ShapeDtypeStruct((nb, KW - 1, CC), F32)],
        scratch_shapes=[pltpu.VMEM((CONF_HDR + Tt, CC), F32)],
        input_output_aliases={len(args) - 1: 0},
        compiler_params=_cp(("parallel", "arbitrary")),
        name="conformer",
    )(*args)
    return o, hn


def _moba_prompt_body(q_ref, k_ref, v_ref, cq_ref, sq_ref, ck_ref, sk_ref, qg_ref, kg_ref, obuf_hbm,
                      o_ref, kh_ref, kbf_ref, vt_ref, km_ref, sel_ref, *, NB):
    del obuf_hbm
    i = pl.program_id(2)
    BK = MOBA_BLOCK

    @pl.when(i == 0)
    def _():
        def blk(n, carry):
            r = pl.multiple_of(n * BK, BK)
            kr = _rope(_rms(k_ref[pl.ds(r, BK), :], kg_ref[...]), ck_ref[pl.ds(r, BK), :], sk_ref[pl.ds(r, BK), :])
            kh_ref[pl.ds(r, BK), :] = kr
            kbf_ref[n] = kr.astype(BF16)
            vt_ref[n] = v_ref[pl.ds(r, BK), :].T.astype(BF16)
            km_ref[pl.ds(n, 1), :] = jnp.sum(kr, axis=0, keepdims=True) * (1.0 / BK)
            return carry
        lax.fori_loop(0, NB, blk, 0)

    qr = _rope(_rms(q_ref[...], qg_ref[...]), cq_ref[...], sq_ref[...])
    gate = _dot3_nt(km_ref[...], qr)
    nidx = lax.broadcasted_iota(I32, gate.shape, 0)
    past = nidx < i
    gm = jnp.where(past, gate, NEG_INF)
    rank = jnp.zeros(gate.shape, I32)
    for m in range(NB):
        grow = gm[m:m + 1, :]
        beats = (grow > gm) | ((grow == gm) & (nidx > m))
        rank = rank + jnp.where(beats, 1, 0)
    sel_ref[...] = jnp.where(past & (rank < MOBA_TOPK), 1.0, 0.0)

    qs = (qr * (HEAD_DIM ** -0.5)).astype(BF16)

    def update(carry, s_list, vt_list):
        m_prev, l_prev, acc = carry
        mx = None
        for s in s_list:
            cur = jnp.max(s, axis=0, keepdims=True)
            mx = cur if mx is None else jnp.maximum(mx, cur)
        m_new = jnp.maximum(m_prev, mx)
        alpha = jnp.exp(m_prev - m_new)
        l_new = alpha * l_prev
        acc = alpha * acc
        for s, vt in zip(s_list, vt_list):
            p = jnp.exp(s - m_new)
            l_new = l_new + jnp.sum(p, axis=0, keepdims=True)
            acc = acc + jnp.dot(vt, p.astype(BF16), preferred_element_type=F32)
        return m_new, l_new, acc

    def pair(pi, carry):
        n0 = 2 * pi
        s_list, vt_list = [], []
        for d in range(2):
            s = _nt(kbf_ref[n0 + d], qs)
            s_list.append(jnp.where(sel_ref[pl.ds(n0 + d, 1), :] > 0.5, s, NEG_INF))
            vt_list.append(vt_ref[n0 + d])
        return update(carry, s_list, vt_list)

    init = (jnp.full((1, BK), NEG_INF, F32), jnp.zeros((1, BK), F32), jnp.zeros((HEAD_DIM, BK), F32))
    carry = lax.fori_loop(0, (i + 1) // 2, pair, init)
    s = _nt(kbf_ref[i], qs)
    kpos = lax.broadcasted_iota(I32, s.shape, 0)
    qpos = lax.broadcasted_iota(I32, s.shape, 1)
    _, l_fin, acc = update(carry, [jnp.where(kpos <= qpos, s, NEG_INF)], [vt_ref[i]])
    o_ref[...] = (acc / l_fin).T.astype(o_ref.dtype)


def _moba_prompt(proj, o_buf, cos_t, sin_t, qg, kg, l, *, lay, B, S, Hm):
    M = proj.shape[0]
    BK = MOBA_BLOCK
    assert S % BK == 0
    NB = S // BK
    MW = Hm * HEAD_DIM
    hd = HEAD_DIM
    vec = lambda b, h, i: (l, 0, 0)
    o, kh = pl.pallas_call(
        functools.partial(_moba_prompt_body, NB=NB),
        grid=(B, Hm, NB),
        in_specs=[pl.BlockSpec((BK, hd), lambda b, h, i: (b * NB + i, lay["MQ"] // hd + h)),
                  pl.BlockSpec((S, hd), lambda b, h, i: (b, lay["MK"] // hd + h)),
                  pl.BlockSpec((S, hd), lambda b, h, i: (b, lay["MV"] // hd + h)),
                  pl.BlockSpec((BK, hd), lambda b, h, i: (i, 0)),
                  pl.BlockSpec((BK, hd), lambda b, h, i: (i, 0)),
                  pl.BlockSpec((S, hd), lambda b, h, i: (0, 0)),
                  pl.BlockSpec((S, hd), lambda b, h, i: (0, 0)),
                  pl.BlockSpec((None, 1, hd), vec),
                  pl.BlockSpec((None, 1, hd), vec),
                  pl.BlockSpec(memory_space=pl.ANY)],
        out_specs=[pl.BlockSpec((BK, hd), lambda b, h, i: (b * NB + i, h)),
                   pl.BlockSpec((None, S, hd), lambda b, h, i: (b, 0, h))],
        out_shape=[jax.ShapeDtypeStruct((M, MW), BF16),
                   jax.ShapeDtypeStruct((B, S, MW), F32)],
        scratch_shapes=[pltpu.VMEM((NB, BK, hd), BF16), pltpu.VMEM((NB, hd, BK), BF16),
                        pltpu.VMEM((NB, hd), F32), pltpu.VMEM((NB, BK), F32)],
        input_output_aliases={9: 0},
        compiler_params=_cp(("parallel", "parallel", "arbitrary")),
        name="moba_prompt",
    )(proj, proj, proj, cos_t, sin_t, cos_t, sin_t, qg, kg, o_buf)
    return o, kh


KM_GROUP = 8


def _kmeans_body(pt_ref, *refs, PPB, PAGE):
    ins, o_ref = refs[:-1], refs[-1]
    for g in range(KM_GROUP):
        s = None
        for p in range(PPB):
            part = jnp.sum(ins[g * PPB + p][...], axis=0)
            s = part if s is None else s + part
        o_ref[g] = s * (1.0 / (PPB * PAGE))


def _cache_block_means(cache_k, pt_flat, *, DB, n_pages):
    L, _, PAGE, Hm, hd = cache_k.shape
    PPB = MOBA_BLOCK // PAGE
    NBK = n_pages // PPB
    assert MOBA_BLOCK % PAGE == 0 and n_pages % PPB == 0 and NBK % KM_GROUP == 0
    PG = KM_GROUP * PPB

    def page_spec(p):
        return pl.BlockSpec((None, None, PAGE, Hm, hd),
                            lambda l, b, g, pt: (l, pt[b * n_pages + g * PG + p], 0, 0, 0))

    return pl.pallas_call(
        functools.partial(_kmeans_body, PPB=PPB, PAGE=PAGE),
        grid_spec=pltpu.PrefetchScalarGridSpec(
            num_scalar_prefetch=1,
            grid=(L, DB, NBK // KM_GROUP),
            in_specs=[page_spec(p) for p in range(PG)],
            out_specs=pl.BlockSpec((None, None, KM_GROUP, Hm, hd), lambda l, b, g, pt: (l, b, g, 0, 0))),
        out_shape=jax.ShapeDtypeStruct((L, DB, NBK, Hm, hd), F32),
        compiler_params=_cp(("parallel", "parallel", "arbitrary")),
        name="cache_block_means",
    )(pt_flat, *([cache_k] * PG))


def _moba_gate_body(qkv_ref, cos_ref, sin_ref, qg_ref, kg_ref, km_ref, q_ref, k_ref, sel_ref, *, Hm, NBK):
    MW = Hm * HEAD_DIM
    T = qkv_ref.shape[0]
    lane_o = lax.broadcasted_iota(I32, (T, LANES), 1)
    lane_g = lax.broadcasted_iota(I32, (T, NBK), 1)
    sel = jnp.zeros((T, LANES), I32)
    for h in range(Hm):
        c0 = h * HEAD_DIM
        qr = _rope(_rms(qkv_ref[:, c0:c0 + HEAD_DIM], qg_ref[...]), cos_ref[...], sin_ref[...])
        kr = _rope(_rms(qkv_ref[:, MW + c0:MW + c0 + HEAD_DIM], kg_ref[...]), cos_ref[...], sin_ref[...])
        q_ref[:, c0:c0 + HEAD_DIM] = qr
        k_ref[:, c0:c0 + HEAD_DIM] = kr
        gate = _dot3_nt(qr, km_ref[:, h, :])
        for j in range(MOBA_TOPK):
            mx = jnp.max(gate, axis=1, keepdims=True)
            idx = jnp.min(jnp.where(gate == mx, lane_g, NBK), axis=1, keepdims=True)
            sel = jnp.where(lane_o == h * MOBA_TOPK + j, idx, sel)
            gate = jnp.where(lane_g == idx, -jnp.inf, gate)
    sel_ref[...] = sel


def _moba_gate(proj, cos_s, sin_s, qg, kg, km, l, *, lay, row0, DB, T, Hm):
    MW = Hm * HEAD_DIM
    NBK = km.shape[2]
    assert row0 % T == 0 and lay["MQ"] == 0 and lay["MK"] == MW and Hm * MOBA_TOPK <= LANES and NBK >= MOBA_TOPK
    vec = lambda b: (l, 0, 0)
    return pl.pallas_call(
        functools.partial(_moba_gate_body, Hm=Hm, NBK=NBK),
        grid=(DB,),
        in_specs=[pl.BlockSpec((T, 3 * MW), lambda b: (row0 // T + b, 0)),
                  pl.BlockSpec((T, HEAD_DIM), lambda b: (0, 0)),
                  pl.BlockSpec((T, HEAD_DIM), lambda b: (0, 0)),
                  pl.BlockSpec((None, 1, HEAD_DIM), vec),
                  pl.BlockSpec((None, 1, HEAD_DIM), vec),
                  pl.BlockSpec((None, None, NBK, Hm, HEAD_DIM), lambda b: (l, b, 0, 0, 0))],
        out_specs=[pl.BlockSpec((T, MW), lambda b: (b, 0)),
                   pl.BlockSpec((T, MW), lambda b: (b, 0)),
                   pl.BlockSpec((None, T, LANES), lambda b: (b, 0, 0))],
        out_shape=[jax.ShapeDtypeStruct((DB * T, MW), F32),
                   jax.ShapeDtypeStruct((DB * T, MW), F32),
                   jax.ShapeDtypeStruct((DB, T, LANES), I32)],
        compiler_params=_cp(("parallel",)),
        name="moba_gate",
    )(proj, cos_s, sin_s, qg, kg, km)


def _moba_sample_body(pt_ref, sel_ref, q_ref, k_ref, v_ref, ck_hbm, cv_hbm, obuf_hbm, o_ref,
                      kbuf_ref, vbuf_ref, sem_ref, *, l, T, Hm, n_pages, PPB, PAGE):
    del obuf_hbm
    b = pl.program_id(0)
    BK = MOBA_BLOCK
    NS = T * MOBA_TOPK
    hd = HEAD_DIM

    def copies(h, buf):
        out = []
        for slot in range(NS):
            t, j = divmod(slot, MOBA_TOPK)
            blk = sel_ref[((b * T + t) * Hm + h) * MOBA_TOPK + j]
            for p in range(PPB):
                page = pt_ref[b * n_pages + blk * PPB + p]
                dst = pl.ds(slot * BK + p * PAGE, PAGE)
                out.append(pltpu.make_async_copy(ck_hbm.at[l, page, :, h, :], kbuf_ref.at[buf, dst, :], sem_ref.at[0, buf]))
                out.append(pltpu.make_async_copy(cv_hbm.at[l, page, :, h, :], vbuf_ref.at[buf, dst, :], sem_ref.at[1, buf]))
        return out

    TP = 2 * SUBLANES
    zpad = jnp.zeros((TP - T, hd), F32)
    span = MOBA_TOPK * BK

    def attend(h, buf):
        c0 = h * hd
        qs = jnp.concatenate([q_ref[:, c0:c0 + hd] * (hd ** -0.5), zpad], axis=0).astype(BF16)
        kn = jnp.concatenate([k_ref[:, c0:c0 + hd], zpad], axis=0).astype(BF16)
        vn = jnp.concatenate([v_ref[:, c0:c0 + hd], zpad], axis=0).astype(BF16)
        s_p = _nt(qs, kbuf_ref[buf].astype(BF16))
        ri = lax.broadcasted_iota(I32, s_p.shape, 0)
        ci = lax.broadcasted_iota(I32, s_p.shape, 1)
        s_p = jnp.where((ci >= ri * span) & (ci < (ri + 1) * span), s_p, NEG_INF)
        s_o = _nt(qs, kn)
        ro = lax.broadcasted_iota(I32, s_o.shape, 0)
        co = lax.broadcasted_iota(I32, s_o.shape, 1)
        s_o = jnp.where((co <= ro) & (co < T), s_o, NEG_INF)
        m = jnp.maximum(jnp.max(s_p, axis=1, keepdims=True), jnp.max(s_o, axis=1, keepdims=True))
        p_p = jnp.exp(s_p - m)
        p_o = jnp.exp(s_o - m)
        den = jnp.sum(p_p, axis=1, keepdims=True) + jnp.sum(p_o, axis=1, keepdims=True)
        o = (jnp.dot(p_p.astype(BF16), vbuf_ref[buf].astype(BF16), preferred_element_type=F32)
             + jnp.dot(p_o.astype(BF16), vn, preferred_element_type=F32)) / den
        o_ref[:, c0:c0 + hd] = o[:T].astype(o_ref.dtype)

    for c in copies(0, 0):
        c.start()
    for h in range(Hm):
        buf = h % 2
        if h + 1 < Hm:
            for c in copies(h + 1, 1 - buf):
                c.start()
        for c in copies(h, buf):
            c.wait()
        attend(h, buf)


def _moba_sample(proj, o_buf, q_rot, k_new, sel_flat, pt_flat, cache_k, cache_v, l, *, lay, row0, DB, T, Hm, n_pages):
    PAGE = cache_k.shape[2]
    PPB = MOBA_BLOCK // PAGE
    NS = T * MOBA_TOPK
    hd = HEAD_DIM
    MW = Hm * hd
    assert T <= 2 * SUBLANES and row0 % T == 0 and lay["MV"] % MW == 0
    grid_spec = pltpu.PrefetchScalarGridSpec(
        num_scalar_prefetch=2,
        grid=(DB,),
        in_specs=[pl.BlockSpec((T, MW), lambda b, pt, sl: (b, 0)),
                  pl.BlockSpec((T, MW), lambda b, pt, sl: (b, 0)),
                  pl.BlockSpec((T, MW), lambda b, pt, sl: (row0 // T + b, lay["MV"] // MW)),
                  pl.BlockSpec(memory_space=pl.ANY),
                  pl.BlockSpec(memory_space=pl.ANY),
                  pl.BlockSpec(memory_space=pl.ANY)],
        out_specs=pl.BlockSpec((T, MW), lambda b, pt, sl: (row0 // T + b, 0)),
        scratch_shapes=[pltpu.VMEM((2, NS * MOBA_BLOCK, hd), F32), pltpu.VMEM((2, NS * MOBA_BLOCK, hd), F32),
                        pltpu.SemaphoreType.DMA((2, 2))])
    return pl.pallas_call(
        functools.partial(_moba_sample_body, l=l, T=T, Hm=Hm, n_pages=n_pages, PPB=PPB, PAGE=PAGE),
        grid_spec=grid_spec,
        out_shape=jax.ShapeDtypeStruct(o_buf.shape, o_buf.dtype),
        input_output_aliases={7: 0},
        compiler_params=_cp(("arbitrary",)),
        name="moba_sample",
    )(pt_flat, sel_flat, q_rot, k_new, proj, cache_k, cache_v, o_buf)


def _layout(GW, MW, CC):
    def up(x, m):
        return (x + m - 1) // m * m
    lay = {"MQ": 0, "MK": MW, "MV": 2 * MW, "GQ": 3 * MW, "GK": 3 * MW + GW, "GV": 3 * MW + 2 * GW,
           "GZ": 3 * MW + 3 * GW}
    lay["BA"] = up(lay["GZ"] + GW, BA_WIDTH)
    lay["GLU"] = up(lay["BA"] + BA_WIDTH, CC)
    lay["NP"] = up(lay["GLU"] + 2 * CC, 4 * LANES)
    return lay


def _arrange_w_in(w_in, lay, GW, MW, CC, Hg):
    L, D, _ = w_in.shape
    o = 0
    qkv_a = w_in[..., o:o + 3 * GW]; o += 3 * GW
    z_a = w_in[..., o:o + GW]; o += GW
    b_a = w_in[..., o:o + Hg]; o += Hg
    a_a = w_in[..., o:o + Hg]; o += Hg
    glu = w_in[..., o:o + 2 * CC]; o += 2 * CC
    qkv_c = w_in[..., o:o + 3 * MW]; o += 3 * MW
    assert o == w_in.shape[-1]
    z = lambda n: jnp.zeros((L, D, n), w_in.dtype)
    parts = [qkv_c, qkv_a, z_a, z(lay["BA"] - lay["GZ"] - GW), b_a, z(LANES - Hg), a_a, z(LANES - Hg),
             z(lay["GLU"] - lay["BA"] - BA_WIDTH), glu, z(lay["NP"] - lay["GLU"] - 2 * CC)]
    return jnp.concatenate(parts, axis=-1).astype(BF16)


def _rope_tables(pos):
    half = HEAD_DIM // 2
    inv_freq = ROPE_THETA ** (-jnp.arange(half, dtype=F32) / half)
    ang = pos.astype(F32)[:, None] * inv_freq[None, :]
    cos, sin = jnp.cos(ang), jnp.sin(ang)
    return jnp.concatenate([cos, cos], axis=-1), jnp.concatenate([-sin, sin], axis=-1)


def _pad_lanes(a):
    L, H = a.shape
    return jnp.pad(a, ((0, 0), (0, LANES - H))).reshape(L, 1, LANES)


def kernel(x_prompt, x_sample, cache_k, cache_v, state_gdn, state_gdn_conv, state_conv, page_table, norm_mix, w_in, gdn_conv_w, gdn_a_log, gdn_dt_bias, gdn_out_norm, conv_dw_w, conv_dw_b, conv_ln_g, conv_ln_b, moba_q_norm, moba_k_norm, w_out, norm_ffn, w_gate, w_up, w_down):
    B, S, D = x_prompt.shape
    DB, T, _ = x_sample.shape
    L = w_in.shape[0]
    Hg = gdn_a_log.shape[1]
    Hm = cache_k.shape[3]
    PAGE = cache_k.shape[2]
    n_pool = cache_k.shape[1]
    n_pages = page_table.shape[1]
    CC = conv_dw_b.shape[1]
    KWG = gdn_conv_w.shape[1]
    KWC = conv_dw_w.shape[1]
    FH = w_gate.shape[2]
    GW, MW = Hg * HEAD_DIM, Hm * HEAD_DIM
    past_len = n_pages * PAGE
    assert cache_k.shape[4] == HEAD_DIM and past_len % MOBA_BLOCK == 0 and GW + CC + MW == D
    assert (past_len + T - 1) // MOBA_BLOCK == past_len // MOBA_BLOCK
    lay = _layout(GW, MW, CC)
    NP = lay["NP"]
    MP, MS = B * S, DB * T
    M = MP + MS

    w_in_r = _arrange_w_in(w_in, lay, GW, MW, CC, Hg)
    wo = w_out.astype(BF16)
    wo_a, wo_b, wo_c = wo[:, :GW], wo[:, GW:GW + CC], wo[:, GW + CC:]
    wg, wu, wd = w_gate.astype(BF16), w_up.astype(BF16), w_down.astype(BF16)
    vec3 = lambda a: a.reshape(L, 1, a.shape[-1])
    norm_mix3, norm_ffn3 = vec3(norm_mix), vec3(norm_ffn)
    og3, qg3, kg3 = vec3(gdn_out_norm), vec3(moba_q_norm), vec3(moba_k_norm)
    dwb3, lng3, lnb3 = vec3(conv_dw_b), vec3(conv_ln_g), vec3(conv_ln_b)
    alog3, dtb3 = _pad_lanes(gdn_a_log), _pad_lanes(gdn_dt_bias)
    cos_p, sin_p = _rope_tables(jnp.arange(S, dtype=I32))
    cos_s, sin_s = _rope_tables(past_len + jnp.arange(T, dtype=I32))
    pt_flat = page_table.reshape(-1).astype(I32)
    gdn_hist0 = jnp.zeros((B, KWG - 1, 3 * GW), F32)
    gdn_state0 = jnp.zeros((B, Hg, HEAD_DIM, HEAD_DIM), F32)
    conv_hist0 = jnp.zeros((B, KWC - 1, CC), F32)

    tm_in = _largest_divisor(M, 1400, SUBLANES)
    tn_in = _largest_divisor(NP, 512, LANES)
    tm_ffn = _largest_divisor(M, 700, SUBLANES)
    th_ffn = _largest_divisor(FH, 512, LANES)
    gdn_c = MOBA_BLOCK if S % MOBA_BLOCK == 0 else _largest_divisor(S, 256, SUBLANES)
    conf_t = _largest_divisor(S, 256, CONF_HDR)

    km_all = _cache_block_means(cache_k, pt_flat, DB=DB, n_pages=n_pages)

    x = jnp.concatenate([x_prompt.reshape(MP, D), x_sample.reshape(MS, D)], axis=0)
    o_a = jnp.zeros((M, GW), BF16)
    o_b = jnp.zeros((M, CC), BF16)
    o_c = jnp.zeros((M, MW), BF16)
    outs = [[] for _ in range(10)]
    for l in range(L):
        proj = _inproj(x, norm_mix3, w_in_r, l, tm_in, tn_in)
        o_a, st_p = _gdn(proj, o_a, gdn_hist0, gdn_state0, gdn_conv_w, alog3, dtb3, og3, l,
                         lay=lay, row0=0, nb=B, T=S, Tt=gdn_c, C=gdn_c, HB=Hg)
        o_a, st_s = _gdn(proj, o_a, state_gdn_conv[l], state_gdn[l], gdn_conv_w, alog3, dtb3, og3, l,
                         lay=lay, row0=MP, nb=DB, T=T, Tt=T, C=LANES, HB=Hg)
        o_b, ch_p = _conformer(proj, o_b, conv_hist0, conv_dw_w, dwb3, lng3, lnb3, l,
                               lay=lay, row0=0, nb=B, T=S, Tt=conf_t)
        o_b, ch_s = _conformer(proj, o_b, state_conv[l], conv_dw_w, dwb3, lng3, lnb3, l,
                               lay=lay, row0=MP, nb=DB, T=T, Tt=T)
        o_c, kh_p = _moba_prompt(proj, o_c, cos_p, sin_p, qg3, kg3, l, lay=lay, B=B, S=S, Hm=Hm)
        q_rot, k_new, sel = _moba_gate(proj, cos_s, sin_s, qg3, kg3, km_all, l,
                                       lay=lay, row0=MP, DB=DB, T=T, Hm=Hm)
        sel_flat = sel[:, :, :Hm * MOBA_TOPK].reshape(-1)
        o_c = _moba_sample(proj, o_c, q_rot, k_new, sel_flat, pt_flat, cache_k, cache_v, l,
                           lay=lay, row0=MP, DB=DB, T=T, Hm=Hm, n_pages=n_pages)
        x = _outproj_ffn(x, o_a, o_b, o_c, wo_a, wo_b, wo_c, norm_ffn3, wg, wu, wd, l, tm_ffn, th_ffn)

        vcols = proj[:, lay["MV"]:lay["MV"] + MW]
        gcols = proj[:, lay["GQ"]:lay["GQ"] + 3 * GW]
        outs[0].append(kh_p.reshape(B, S, Hm, HEAD_DIM))
        outs[1].append(vcols[:MP].reshape(B, S, Hm, HEAD_DIM))
        outs[2].append(k_new.reshape(DB, T, Hm, HEAD_DIM))
        outs[3].append(vcols[MP:].reshape(DB, T, Hm, HEAD_DIM))
        outs[4].append(st_p)
        outs[5].append(st_s)
        outs[6].append(gcols[:MP].reshape(B, S, 3 * GW)[:, S - (KWG - 1):])
        hist_ext = jnp.concatenate([state_gdn_conv[l], gcols[MP:].reshape(DB, T, 3 * GW)], axis=1)
        outs[7].append(hist_ext[:, -(KWG - 1):])
        outs[8].append(ch_p)
        outs[9].append(ch_s)

    y_prompt = x[:MP].reshape(B, S, D)
    y_sample = x[MP:].reshape(DB, T, D)
    return (y_prompt, y_sample) + tuple(jnp.stack(o) for o in outs)
```

```python
import functools

import jax
import jax.numpy as jnp
from jax import lax
from jax.experimental import pallas as pl
from jax.experimental.pallas import tpu as pltpu

F32 = jnp.float32
BF16 = jnp.bfloat16
I32 = jnp.int32

EPS = 1e-6
NEG_INF = -1e30
HEAD_DIM = 128
MOBA_BLOCK = 256
MOBA_TOPK = 3
ROPE_THETA = 10000.0
LANES = 128
SUBLANES = 8
BA_WIDTH = 2 * LANES
VMEM_LIMIT = 60 * 1024 * 1024


def _cp(sem, vmem=VMEM_LIMIT):
    return pltpu.CompilerParams(dimension_semantics=sem, vmem_limit_bytes=vmem)


def _bdot(a, b):
    return jnp.dot(a.astype(BF16), b.astype(BF16), preferred_element_type=F32)


def _nt(a, b):
    return lax.dot_general(a, b, (((1,), (1,)), ((), ())), preferred_element_type=F32)


def _bdot_nt(a, b):
    return _nt(a.astype(BF16), b.astype(BF16))


def _split_bf16(a):
    hi = a.astype(BF16)
    lo = (a - hi.astype(F32)).astype(BF16)
    return hi, lo


def _dot3_nt(a, b):
    ah, al = _split_bf16(a)
    bh, bl = _split_bf16(b)
    return _nt(ah, bh) + (_nt(ah, bl) + _nt(al, bh))


def _sigmoid(x):
    return 1.0 / (1.0 + jnp.exp(-x))


def _silu(x):
    return x * _sigmoid(x)


def _softplus(x):
    return jnp.maximum(x, 0.0) + jnp.log(1.0 + jnp.exp(-jnp.abs(x)))


def _rms(x, g):
    return x * lax.rsqrt(jnp.mean(x * x, axis=-1, keepdims=True) + EPS) * g


def _rope(x, cos_full, sin_signed):
    return x * cos_full + pltpu.roll(x, HEAD_DIM // 2, 1) * sin_signed


def _largest_divisor(n, cap, mult):
    best = None
    for d in range(mult, min(n, cap) + 1, mult):
        if n % d == 0:
            best = d
    assert best is not None, (n, cap, mult)
    return best


def _inproj_body(x_ref, g_ref, w_ref, o_ref, xn_ref):
    @pl.when(pl.program_id(1) == 0)
    def _():
        xn_ref[...] = _rms(x_ref[...], g_ref[...]).astype(BF16)

    o_ref[...] = jnp.dot(xn_ref[...], w_ref[...], preferred_element_type=F32)


def _inproj(x, g, w, l, tm, tn):
    M, D = x.shape
    NP = w.shape[-1]
    return pl.pallas_call(
        _inproj_body,
        grid=(M // tm, NP // tn),
        in_specs=[pl.BlockSpec((tm, D), lambda i, j: (i, 0)),
                  pl.BlockSpec((None, 1, D), lambda i, j: (l, 0, 0)),
                  pl.BlockSpec((None, D, tn), lambda i, j: (l, 0, j))],
        out_specs=pl.BlockSpec((tm, tn), lambda i, j: (i, j)),
        out_shape=jax.ShapeDtypeStruct((M, NP), F32),
        scratch_shapes=[pltpu.VMEM((tm, D), BF16)],
        compiler_params=_cp(("parallel", "arbitrary")),
        name="inproj",
    )(x, g, w)


def _ffn_body(x_ref, oa_ref, ob_ref, oc_ref, wa_ref, wb_ref, wc_ref, g_ref, wg_ref, wu_ref, wd_ref,
              y_ref, hn_ref):
    @pl.when(pl.program_id(1) == 0)
    def _():
        h = (x_ref[...]
             + jnp.dot(oa_ref[...], wa_ref[...], preferred_element_type=F32)
             + jnp.dot(ob_ref[...], wb_ref[...], preferred_element_type=F32)
             + jnp.dot(oc_ref[...], wc_ref[...], preferred_element_type=F32))
        y_ref[...] = h
        hn_ref[...] = _rms(h, g_ref[...]).astype(BF16)

    hn = hn_ref[...]
    a = jnp.dot(hn, wg_ref[...], preferred_element_type=F32)
    b = jnp.dot(hn, wu_ref[...], preferred_element_type=F32)
    t = (_silu(a) * b).astype(BF16)
    y_ref[...] += jnp.dot(t, wd_ref[...], preferred_element_type=F32)


def _outproj_ffn(x, oa, ob, oc, wa, wb, wc, g, wg, wu, wd, l, tm, th):
    M, D = x.shape
    FH = wg.shape[-1]
    GW, CC, MW = oa.shape[1], ob.shape[1], oc.shape[1]
    row = lambda i, j: (i, 0)
    return pl.pallas_call(
        _ffn_body,
        grid=(M // tm, FH // th),
        in_specs=[pl.BlockSpec((tm, D), row),
                  pl.BlockSpec((tm, GW), row),
                  pl.BlockSpec((tm, CC), row),
                  pl.BlockSpec((tm, MW), row),
                  pl.BlockSpec((None, GW, D), lambda i, j: (l, 0, 0), pipeline_mode=pl.Buffered(1)),
                  pl.BlockSpec((None, CC, D), lambda i, j: (l, 0, 0), pipeline_mode=pl.Buffered(1)),
                  pl.BlockSpec((None, MW, D), lambda i, j: (l, 0, 0), pipeline_mode=pl.Buffered(1)),
                  pl.BlockSpec((None, 1, D), lambda i, j: (l, 0, 0)),
                  pl.BlockSpec((None, D, th), lambda i, j: (l, 0, j)),
                  pl.BlockSpec((None, D, th), lambda i, j: (l, 0, j)),
                  pl.BlockSpec((None, th, D), lambda i, j: (l, j, 0))],
        out_specs=pl.BlockSpec((tm, D), row),
        out_shape=jax.ShapeDtypeStruct((M, D), F32),
        scratch_shapes=[pltpu.VMEM((tm, D), BF16)],
        compiler_params=_cp(("parallel", "arbitrary")),
        name="outproj_ffn",
    )(x, oa, ob, oc, wa, wb, wc, g, wg, wu, wd)


def _gdn_chunks(qs, ks, vs, betas, gs, Ss):
    nh = len(qs)
    hs = range(nh)
    C, dk = qs[0].shape
    assert C & (C - 1) == 0
    ri = lax.broadcasted_iota(I32, (C, C), 0)
    ci = lax.broadcasted_iota(I32, (C, C), 1)
    incl = ri >= ci
    eye = ri == ci
    xor = ri ^ ci
    g_row = [jnp.sum(jnp.where(eye, gs[h], 0.0), axis=0, keepdims=True) for h in hs]
    gc_row = [jnp.sum(jnp.where(ri <= ci, gs[h], 0.0), axis=0, keepdims=True) for h in hs]
    gc_col = [jnp.sum(jnp.where(incl, g_row[h], 0.0), axis=1, keepdims=True) for h in hs]
    decay = [jnp.where(incl, jnp.exp(jnp.where(incl, gc_col[h] - gc_row[h], 0.0)), 0.0) for h in hs]
    kbf = [ks[h].astype(BF16) for h in hs]
    kb = [ks[h] * betas[h] for h in hs]
    a = [jnp.where(ri > ci, _nt(kb[h].astype(BF16), kbf[h]) * decay[h], 0.0).astype(BF16) for h in hs]
    p = [jnp.where(eye, 1.0, 0.0) - jnp.where(xor == 1, a[h].astype(F32), 0.0) for h in hs]
    for lvl in range(1, C.bit_length() - 1):
        off = jnp.where((xor >> lvl) == 1, 1.0, 0.0).astype(BF16)
        pb = [p[h].astype(BF16) for h in hs]
        t1 = [jnp.dot(a[h] * off, pb[h], preferred_element_type=F32).astype(BF16) for h in hs]
        p = [p[h] - jnp.dot(pb[h], t1[h], preferred_element_type=F32) for h in hs]
    pb = [p[h].astype(BF16) for h in hs]
    egc = [jnp.exp(gc_col[h]) for h in hs]
    u = [jnp.dot(pb[h], (vs[h] * betas[h]).astype(BF16), preferred_element_type=F32) for h in hs]
    w = [jnp.dot(pb[h], (kb[h] * egc[h]).astype(BF16), preferred_element_type=F32) for h in hs]
    sb = [Ss[h].astype(BF16) for h in hs]
    v_new = [u[h] - jnp.dot(w[h].astype(BF16), sb[h], preferred_element_type=F32) for h in hs]
    vnb = [v_new[h].astype(BF16) for h in hs]
    qsc = [qs[h] * (dk ** -0.5) for h in hs]
    intra = [jnp.where(incl, _nt(qsc[h].astype(BF16), kbf[h]) * decay[h], 0.0).astype(BF16) for h in hs]
    o = [jnp.dot((qsc[h] * egc[h]).astype(BF16), sb[h], preferred_element_type=F32)
         + jnp.dot(intra[h], vnb[h], preferred_element_type=F32) for h in hs]
    g_last = [gc_col[h][C - 1:C, :] for h in hs]
    kdec_t = [(ks[h] * jnp.exp(g_last[h] - gc_col[h])).T.astype(BF16) for h in hs]
    s_new = [Ss[h] * jnp.exp(g_last[h]) + jnp.dot(kdec_t[h], vnb[h], preferred_element_type=F32) for h in hs]
    return o, s_new


def _gdn_body(q_ref, k_ref, v_ref, z_ref, ba_ref, hq_ref, hk_ref, hv_ref, s0_ref,
              wq_ref, wk_ref, wv_ref, alog_ref, dtb_ref, og_ref, obuf_hbm,
              o_ref, sout_ref, xq_ref, xk_ref, xv_ref, s_ref, *, Tt, C, HB, H, KW, NT):
    del obuf_hbm
    hb = pl.program_id(1)
    t = pl.program_id(2)
    HR = SUBLANES
    xrefs = (xq_ref, xk_ref, xv_ref)

    @pl.when(t == 0)
    def _():
        for xr, hr in zip(xrefs, (hq_ref, hk_ref, hv_ref)):
            xr[HR - (KW - 1):HR, :] = hr[...]
        s_ref[...] = s0_ref[...]

    for xr, r in zip(xrefs, (q_ref, k_ref, v_ref)):
        xr[HR:HR + Tt, :] = r[...]
        if C > Tt:
            xr[HR + Tt:HR + C, :] = jnp.zeros((C - Tt, xr.shape[1]), F32)

    def conv(xr, w_ref, s):
        acc = None
        for j in range(KW):
            term = xr[HR - (KW - 1) + j:HR - (KW - 1) + j + C, s * LANES:(s + 1) * LANES] \
                * w_ref[j:j + 1, s * LANES:(s + 1) * LANES]
            acc = term if acc is None else acc + term
        return acc

    ba = ba_ref[...]
    if C > Tt:
        ba = jnp.concatenate([ba, jnp.zeros((C - Tt, ba.shape[1]), F32)], axis=0)
    beta_all = _sigmoid(ba[:, :LANES])
    g_all = -jnp.exp(alog_ref[...]) * _softplus(ba[:, LANES:] + dtb_ref[...])
    lane = lax.broadcasted_iota(I32, (1, LANES), 1)
    rvalid = lax.broadcasted_iota(I32, (C, 1), 0) < Tt

    qs, ks, vs, betas, gs = [], [], [], [], []
    for s in range(HB):
        hsel = lane == hb * HB + s
        beta = jnp.sum(jnp.where(hsel, beta_all, 0.0), axis=1, keepdims=True)
        g = jnp.sum(jnp.where(hsel, g_all, 0.0), axis=1, keepdims=True)
        q = _silu(conv(xq_ref, wq_ref, s))
        k = _silu(conv(xk_ref, wk_ref, s))
        v = _silu(conv(xv_ref, wv_ref, s))
        q = q * lax.rsqrt(jnp.sum(q * q, axis=-1, keepdims=True) + EPS)
        k = k * lax.rsqrt(jnp.sum(k * k, axis=-1, keepdims=True) + EPS)
        if C > Tt:
            q = jnp.where(rvalid, q, 0.0)
            k = jnp.where(rvalid, k, 0.0)
            v = jnp.where(rvalid, v, 0.0)
            beta = jnp.where(rvalid, beta, 0.0)
            g = jnp.where(rvalid, g, 0.0)
        for lst, val in zip((qs, ks, vs, betas, gs), (q, k, v, beta, g)):
            lst.append(val)
    os_, s_new = _gdn_chunks(qs, ks, vs, betas, gs, [s_ref[s] for s in range(HB)])
    for s in range(HB):
        s_ref[s] = s_new[s]
        zz = z_ref[:, s * LANES:(s + 1) * LANES]
        o_ref[:, s * LANES:(s + 1) * LANES] = (_rms(os_[s][:Tt], og_ref[...]) * _silu(zz)).astype(o_ref.dtype)

    if NT > 1:
        for xr in xrefs:
            xr[0:HR, :] = xr[C:C + HR, :]

    @pl.when(t == NT - 1)
    def _():
        sout_ref[...] = s_ref[...]


def _gdn(proj, o_buf, hist, s0, conv_w, alog, dtb, og, l, *, lay, row0, nb, T, Tt, C, HB):
    M = proj.shape[0]
    H = s0.shape[1]
    GW = H * HEAD_DIM
    KW = conv_w.shape[1]
    NT = T // Tt
    W = HB * LANES
    assert T % Tt == 0 and row0 % Tt == 0 and H % HB == 0 and (NT == 1 or Tt == C)
    rb0 = row0 // Tt

    def col(off):
        assert off % W == 0
        return lambda b, h, t: (rb0 + b * NT + t, off // W + h)

    def hcol(off):
        return lambda b, h, t: (b, 0, off // W + h)

    def wcol(off):
        return lambda b, h, t: (l, 0, off // W + h)

    vec = lambda b, h, t: (l, 0, 0)
    in_specs = [pl.BlockSpec((Tt, W), col(lay["GQ"])),
                pl.BlockSpec((Tt, W), col(lay["GK"])),
                pl.BlockSpec((Tt, W), col(lay["GV"])),
                pl.BlockSpec((Tt, W), col(lay["GZ"])),
                pl.BlockSpec((Tt, BA_WIDTH), lambda b, h, t: (rb0 + b * NT + t, lay["BA"] // BA_WIDTH)),
                pl.BlockSpec((None, KW - 1, W), hcol(0)),
                pl.BlockSpec((None, KW - 1, W), hcol(GW)),
                pl.BlockSpec((None, KW - 1, W), hcol(2 * GW)),
                pl.BlockSpec((None, HB, HEAD_DIM, HEAD_DIM), lambda b, h, t: (b, h, 0, 0)),
                pl.BlockSpec((None, KW, W), wcol(0)),
                pl.BlockSpec((None, KW, W), wcol(GW)),
                pl.BlockSpec((None, KW, W), wcol(2 * GW)),
                pl.BlockSpec((None, 1, LANES), vec),
                pl.BlockSpec((None, 1, LANES), vec),
                pl.BlockSpec((None, 1, HEAD_DIM), vec),
                pl.BlockSpec(memory_space=pl.ANY)]
    args = [proj, proj, proj, proj, proj, hist, hist, hist, s0, conv_w, conv_w, conv_w, alog, dtb, og, o_buf]
    assert o_buf.shape == (M, GW) and o_buf.dtype == BF16
    o, st = pl.pallas_call(
        functools.partial(_gdn_body, Tt=Tt, C=C, HB=HB, H=H, KW=KW, NT=NT),
        grid=(nb, H // HB, NT),
        in_specs=in_specs,
        out_specs=[pl.BlockSpec((Tt, W), lambda b, h, t: (rb0 + b * NT + t, h)),
                   pl.BlockSpec((None, HB, HEAD_DIM, HEAD_DIM), lambda b, h, t: (b, h, 0, 0))],
        out_shape=[jax.ShapeDtypeStruct((M, GW), BF16),
                   jax.ShapeDtypeStruct((nb, H, HEAD_DIM, HEAD_DIM), F32)],
        scratch_shapes=[pltpu.VMEM((SUBLANES + C, W), F32)] * 3 + [pltpu.VMEM((HB, HEAD_DIM, HEAD_DIM), F32)],
        input_output_aliases={len(args) - 1: 0},
        compiler_params=_cp(("parallel", "parallel", "arbitrary")),
        name="gdn",
    )(*args)
    return o, st


CONF_HDR = 32
CONF_SUB = 32


def _conf_body(a_ref, gt_ref, hist_ref, w_ref, b_ref, lg_ref, lb_ref, obuf_hbm, o_ref, hout_ref, buf_ref, sh_ref,
               *, Tt, KW, NT):
    del obuf_hbm
    t = pl.program_id(1)
    HR = CONF_HDR
    base = HR - (KW - 1)

    @pl.when(t == 0)
    def _():
        buf_ref[0:HR, :] = jnp.zeros((HR, buf_ref.shape[1]), F32)
        buf_ref[base:HR, :] = hist_ref[...]

    buf_ref[HR:HR + Tt, :] = a_ref[...] * _sigmoid(gt_ref[...])
    span = HR + Tt - SUBLANES
    for r in range(1, SUBLANES):
        sh_ref[r - 1, 0:span, :] = buf_ref[r:r + span, :]

    def tap_rows(j, r0, n):
        a, r = divmod(base + j, SUBLANES)
        lo = r0 + a * SUBLANES
        return buf_ref[lo:lo + n, :] if r == 0 else sh_ref[r - 1, lo:lo + n, :]

    sub = min(CONF_SUB, Tt)
    for r0 in range(0, Tt, sub):
        acc = None
        for j in range(KW):
            term = tap_rows(j, r0, sub) * w_ref[j:j + 1, :]
            acc = term if acc is None else acc + term
        y = acc + b_ref[...]
        yc = y - jnp.mean(y, axis=-1, keepdims=True)
        yn = yc * lax.rsqrt(jnp.mean(yc * yc, axis=-1, keepdims=True) + EPS) * lg_ref[...] + lb_ref[...]
        o_ref[r0:r0 + sub, :] = _silu(yn).astype(o_ref.dtype)

    @pl.when(t == NT - 1)
    def _():
        hout_ref[...] = buf_ref[Tt + base:Tt + HR, :]

    if NT > 1:
        buf_ref[0:HR, :] = buf_ref[Tt:Tt + HR, :]


def _conformer(proj, o_buf, hist, dw_w, dw_b, ln_g, ln_b, l, *, lay, row0, nb, T, Tt):
    M = proj.shape[0]
    CC = dw_w.shape[2]
    KW = dw_w.shape[1]
    NT = T // Tt
    assert T % Tt == 0 and row0 % Tt == 0 and lay["GLU"] % CC == 0 and KW - 1 <= CONF_HDR
    assert NT == 1 or Tt >= CONF_HDR
    rb0 = row0 // Tt
    cb = lay["GLU"] // CC
    vec = lambda b, t: (l, 0, 0)
    in_specs = [pl.BlockSpec((Tt, CC), lambda b, t: (rb0 + b * NT + t, cb)),
                pl.BlockSpec((Tt, CC), lambda b, t: (rb0 + b * NT + t, cb + 1)),
                pl.BlockSpec((None, KW - 1, CC), lambda b, t: (b, 0, 0)),
                pl.BlockSpec((None, KW, CC), vec),
                pl.BlockSpec((None, 1, CC), vec),
                pl.BlockSpec((None, 1, CC), vec),
                pl.BlockSpec((None, 1, CC), vec),
                pl.BlockSpec(memory_space=pl.ANY)]
    args = [proj, proj, hist, dw_w, dw_b, ln_g, ln_b, o_buf]
    assert o_buf.shape == (M, CC) and o_buf.dtype == BF16
    o, hn = pl.pallas_call(
        functools.partial(_conf_body, Tt=Tt, KW=KW, NT=NT),
        grid=(nb, NT),
        in_specs=in_specs,
        out_specs=[pl.BlockSpec((Tt, CC), lambda b, t: (rb0 + b * NT + t, 0)),
                   pl.BlockSpec((None, KW - 1, CC), lambda b, t: (b, 0, 0))],
        out_shape=[jax.ShapeDtypeStruct((M, CC), BF16),
                   jax.ShapeDtypeStruct((nb, KW - 1, CC), F32)],
        scratch_shapes=[pltpu.VMEM((CONF_HDR + Tt, CC), F32),
                        pltpu.VMEM((SUBLANES - 1, CONF_HDR + Tt, CC), F32)],
        input_output_aliases={len(args) - 1: 0},
        compiler_params=_cp(("parallel", "arbitrary")),
        name="conformer",
    )(*args)
    return o, hn


def _moba_prompt_body(q_ref, k_ref, v_ref, cq_ref, sq_ref, ck_ref, sk_ref, qg_ref, kg_ref,
                      obuf_hbm, khbuf_hbm, vhbuf_hbm,
                      o_ref, kh_ref, vh_ref, kbf_ref, vt_ref, km_ref, sel_ref, *, NB):
    del obuf_hbm, khbuf_hbm, vhbuf_hbm
    i = pl.program_id(2)
    BK = MOBA_BLOCK

    @pl.when(i == 0)
    def _():
        def blk(n, carry):
            r = pl.multiple_of(n * BK, BK)
            kr = _rope(_rms(k_ref[pl.ds(r, BK), :], kg_ref[...]), ck_ref[pl.ds(r, BK), :], sk_ref[pl.ds(r, BK), :])
            vb = v_ref[pl.ds(r, BK), :]
            kh_ref[pl.ds(r, BK), :] = kr
            vh_ref[pl.ds(r, BK), :] = vb
            kbf_ref[n] = kr.astype(BF16)
            vt_ref[n] = vb.T.astype(BF16)
            km_ref[pl.ds(n, 1), :] = jnp.sum(kr, axis=0, keepdims=True) * (1.0 / BK)
            return carry
        lax.fori_loop(0, NB, blk, 0)

    qr = _rope(_rms(q_ref[...], qg_ref[...]), cq_ref[...], sq_ref[...])
    gate = _dot3_nt(km_ref[...], qr)
    nidx = lax.broadcasted_iota(I32, gate.shape, 0)
    past = nidx < i
    gm = jnp.where(past, gate, NEG_INF)
    rank = jnp.zeros(gate.shape, I32)
    for m in range(NB):
        grow = gm[m:m + 1, :]
        beats = (grow > gm) | ((grow == gm) & (nidx > m))
        rank = rank + jnp.where(beats, 1, 0)
    sel_ref[...] = jnp.where(past & (rank < MOBA_TOPK), 1.0, 0.0)

    qs = (qr * (HEAD_DIM ** -0.5)).astype(BF16)

    def update(carry, s_list, vt_list):
        m_prev, l_prev, acc = carry
        mx = None
        for s in s_list:
            cur = jnp.max(s, axis=0, keepdims=True)
            mx = cur if mx is None else jnp.maximum(mx, cur)
        m_new = jnp.maximum(m_prev, mx)
        alpha = jnp.exp(m_prev - m_new)
        l_new = alpha * l_prev
        acc = alpha * acc
        for s, vt in zip(s_list, vt_list):
            p = jnp.exp(s - m_new)
            l_new = l_new + jnp.sum(p, axis=0, keepdims=True)
            acc = acc + jnp.dot(vt, p.astype(BF16), preferred_element_type=F32)
        return m_new, l_new, acc

    def group(n0, width, carry):
        s_list, vt_list = [], []
        for d in range(width):
            s = _nt(kbf_ref[n0 + d], qs)
            s_list.append(jnp.where(sel_ref[pl.ds(n0 + d, 1), :] > 0.5, s, NEG_INF))
            vt_list.append(vt_ref[n0 + d])
        return update(carry, s_list, vt_list)

    init = (jnp.full((1, BK), NEG_INF, F32), jnp.zeros((1, BK), F32), jnp.zeros((HEAD_DIM, BK), F32))
    nquad = i // 4
    carry = lax.fori_loop(0, nquad, lambda g, c: group(4 * g, 4, c), init)
    carry = lax.fori_loop(0, (i - 4 * nquad + 1) // 2, lambda g, c: group(4 * nquad + 2 * g, 2, c), carry)
    s = _nt(kbf_ref[i], qs)
    kpos = lax.broadcasted_iota(I32, s.shape, 0)
    qpos = lax.broadcasted_iota(I32, s.shape, 1)
    _, l_fin, acc = update(carry, [jnp.where(kpos <= qpos, s, NEG_INF)], [vt_ref[i]])
    o_ref[...] = (acc / l_fin).T.astype(o_ref.dtype)


def _moba_prompt(proj, o_buf, kh_buf, vh_buf, cos_t, sin_t, qg, kg, l, *, lay, B, S, Hm):
    M = proj.shape[0]
    BK = MOBA_BLOCK
    assert S % BK == 0
    NB = S // BK
    MW = Hm * HEAD_DIM
    hd = HEAD_DIM
    vec = lambda b, h, i: (l, 0, 0)
    L = kh_buf.shape[0]
    assert kh_buf.shape == vh_buf.shape == (L, B, Hm, S, hd) and kh_buf.dtype == vh_buf.dtype == F32
    kv_out = pl.BlockSpec((None, None, None, S, hd), lambda b, h, i: (l, b, h, 0, 0))
    return pl.pallas_call(
        functools.partial(_moba_prompt_body, NB=NB),
        grid=(B, Hm, NB),
        in_specs=[pl.BlockSpec((BK, hd), lambda b, h, i: (b * NB + i, lay["MQ"] // hd + h)),
                  pl.BlockSpec((S, hd), lambda b, h, i: (b, lay["MK"] // hd + h)),
                  pl.BlockSpec((S, hd), lambda b, h, i: (b, lay["MV"] // hd + h)),
                  pl.BlockSpec((BK, hd), lambda b, h, i: (i, 0)),
                  pl.BlockSpec((BK, hd), lambda b, h, i: (i, 0)),
                  pl.BlockSpec((S, hd), lambda b, h, i: (0, 0)),
                  pl.BlockSpec((S, hd), lambda b, h, i: (0, 0)),
                  pl.BlockSpec((None, 1, hd), vec),
                  pl.BlockSpec((None, 1, hd), vec),
                  pl.BlockSpec(memory_space=pl.ANY),
                  pl.BlockSpec(memory_space=pl.ANY),
                  pl.BlockSpec(memory_space=pl.ANY)],
        out_specs=[pl.BlockSpec((BK, hd), lambda b, h, i: (b * NB + i, h)), kv_out, kv_out],
        out_shape=[jax.ShapeDtypeStruct((M, MW), BF16),
                   jax.ShapeDtypeStruct(kh_buf.shape, F32),
                   jax.ShapeDtypeStruct(vh_buf.shape, F32)],
        scratch_shapes=[pltpu.VMEM((NB, BK, hd), BF16), pltpu.VMEM((NB, hd, BK), BF16),
                        pltpu.VMEM((NB, hd), F32), pltpu.VMEM((NB, BK), F32)],
        input_output_aliases={9: 0, 10: 1, 11: 2},
        compiler_params=_cp(("parallel", "parallel", "arbitrary")),
        name="moba_prompt",
    )(proj, proj, proj, cos_t, sin_t, cos_t, sin_t, qg, kg, o_buf, kh_buf, vh_buf)


KM_GROUP = 8


def _kmeans_body(pt_ref, *refs, PPB, PAGE):
    ins, o_ref = refs[:-1], refs[-1]
    for g in range(KM_GROUP):
        s = None
        for p in range(PPB):
            part = jnp.sum(ins[g * PPB + p][...], axis=1)
            s = part if s is None else s + part
        o_ref[g] = s * (1.0 / (PPB * PAGE))


def _cache_block_means(cache_k, pt_flat, *, DB, n_pages):
    L, _, Hm, PAGE, hd = cache_k.shape
    PPB = MOBA_BLOCK // PAGE
    NBK = n_pages // PPB
    assert MOBA_BLOCK % PAGE == 0 and n_pages % PPB == 0 and NBK % KM_GROUP == 0
    PG = KM_GROUP * PPB

    def page_spec(p):
        return pl.BlockSpec((None, None, Hm, PAGE, hd),
                            lambda l, b, g, pt: (l, pt[b * n_pages + g * PG + p], 0, 0, 0))

    return pl.pallas_call(
        functools.partial(_kmeans_body, PPB=PPB, PAGE=PAGE),
        grid_spec=pltpu.PrefetchScalarGridSpec(
            num_scalar_prefetch=1,
            grid=(L, DB, NBK // KM_GROUP),
            in_specs=[page_spec(p) for p in range(PG)],
            out_specs=pl.BlockSpec((None, None, KM_GROUP, Hm, hd), lambda l, b, g, pt: (l, b, g, 0, 0))),
        out_shape=jax.ShapeDtypeStruct((L, DB, NBK, Hm, hd), F32),
        compiler_params=_cp(("parallel", "parallel", "arbitrary")),
        name="cache_block_means",
    )(pt_flat, *([cache_k] * PG))


def _moba_gate_body(qkv_ref, cos_ref, sin_ref, qg_ref, kg_ref, km_ref, q_ref, k_ref, sel_ref, *, Hm, NBK):
    MW = Hm * HEAD_DIM
    T = qkv_ref.shape[0]
    lane_o = lax.broadcasted_iota(I32, (T, LANES), 1)
    lane_g = lax.broadcasted_iota(I32, (T, NBK), 1)
    sel = jnp.zeros((T, LANES), I32)
    for h in range(Hm):
        c0 = h * HEAD_DIM
        qr = _rope(_rms(qkv_ref[:, c0:c0 + HEAD_DIM], qg_ref[...]), cos_ref[...], sin_ref[...])
        kr = _rope(_rms(qkv_ref[:, MW + c0:MW + c0 + HEAD_DIM], kg_ref[...]), cos_ref[...], sin_ref[...])
        q_ref[:, c0:c0 + HEAD_DIM] = qr
        k_ref[:, c0:c0 + HEAD_DIM] = kr
        gate = _dot3_nt(qr, km_ref[:, h, :])
        for j in range(MOBA_TOPK):
            mx = jnp.max(gate, axis=1, keepdims=True)
            idx = jnp.min(jnp.where(gate == mx, lane_g, NBK), axis=1, keepdims=True)
            sel = jnp.where(lane_o == h * MOBA_TOPK + j, idx, sel)
            gate = jnp.where(lane_g == idx, -jnp.inf, gate)
    sel_ref[...] = sel


def _moba_gate(proj, cos_s, sin_s, qg, kg, km, l, *, lay, row0, DB, T, Hm):
    MW = Hm * HEAD_DIM
    NBK = km.shape[2]
    assert row0 % T == 0 and lay["MQ"] == 0 and lay["MK"] == MW and Hm * MOBA_TOPK <= LANES and NBK >= MOBA_TOPK
    vec = lambda b: (l, 0, 0)
    return pl.pallas_call(
        functools.partial(_moba_gate_body, Hm=Hm, NBK=NBK),
        grid=(DB,),
        in_specs=[pl.BlockSpec((T, 3 * MW), lambda b: (row0 // T + b, 0)),
                  pl.BlockSpec((T, HEAD_DIM), lambda b: (0, 0)),
                  pl.BlockSpec((T, HEAD_DIM), lambda b: (0, 0)),
                  pl.BlockSpec((None, 1, HEAD_DIM), vec),
                  pl.BlockSpec((None, 1, HEAD_DIM), vec),
                  pl.BlockSpec((None, None, NBK, Hm, HEAD_DIM), lambda b: (l, b, 0, 0, 0))],
        out_specs=[pl.BlockSpec((T, MW), lambda b: (b, 0)),
                   pl.BlockSpec((T, MW), lambda b: (b, 0)),
                   pl.BlockSpec((None, T, LANES), lambda b: (b, 0, 0))],
        out_shape=[jax.ShapeDtypeStruct((DB * T, MW), F32),
                   jax.ShapeDtypeStruct((DB * T, MW), F32),
                   jax.ShapeDtypeStruct((DB, T, LANES), I32)],
        compiler_params=_cp(("parallel",)),
        name="moba_gate",
    )(proj, cos_s, sin_s, qg, kg, km)


def _moba_sample_body(pt_ref, sel_ref, q_ref, k_ref, v_ref, ck_hbm, cv_hbm, obuf_hbm, o_ref,
                      kbuf_ref, vbuf_ref, sem_ref, *, l, T, Hm, n_pages, PPB, PAGE):
    del obuf_hbm
    b = pl.program_id(0)
    BK = MOBA_BLOCK
    NS = T * MOBA_TOPK
    hd = HEAD_DIM

    def copies(h, buf):
        out = []
        for slot in range(NS):
            t, j = divmod(slot, MOBA_TOPK)
            blk = sel_ref[((b * T + t) * Hm + h) * MOBA_TOPK + j]
            for p in range(PPB):
                page = pt_ref[b * n_pages + blk * PPB + p]
                dst = pl.ds(slot * BK + p * PAGE, PAGE)
                out.append(pltpu.make_async_copy(ck_hbm.at[l, page, h], kbuf_ref.at[buf, dst, :], sem_ref.at[0, buf]))
                out.append(pltpu.make_async_copy(cv_hbm.at[l, page, h], vbuf_ref.at[buf, dst, :], sem_ref.at[1, buf]))
        return out

    TP = 2 * SUBLANES
    zpad = jnp.zeros((TP - T, hd), F32)
    span = MOBA_TOPK * BK

    def attend(h, buf):
        c0 = h * hd
        qs = jnp.concatenate([q_ref[:, c0:c0 + hd] * (hd ** -0.5), zpad], axis=0).astype(BF16)
        kn = jnp.concatenate([k_ref[:, c0:c0 + hd], zpad], axis=0).astype(BF16)
        vn = jnp.concatenate([v_ref[:, c0:c0 + hd], zpad], axis=0).astype(BF16)
        s_p = _nt(qs, kbuf_ref[buf].astype(BF16))
        ri = lax.broadcasted_iota(I32, s_p.shape, 0)
        ci = lax.broadcasted_iota(I32, s_p.shape, 1)
        s_p = jnp.where((ci >= ri * span) & (ci < (ri + 1) * span), s_p, NEG_INF)
        s_o = _nt(qs, kn)
        ro = lax.broadcasted_iota(I32, s_o.shape, 0)
        co = lax.broadcasted_iota(I32, s_o.shape, 1)
        s_o = jnp.where((co <= ro) & (co < T), s_o, NEG_INF)
        m = jnp.maximum(jnp.max(s_p, axis=1, keepdims=True), jnp.max(s_o, axis=1, keepdims=True))
        p_p = jnp.exp(s_p - m)
        p_o = jnp.exp(s_o - m)
        den = jnp.sum(p_p, axis=1, keepdims=True) + jnp.sum(p_o, axis=1, keepdims=True)
        o = (jnp.dot(p_p.astype(BF16), vbuf_ref[buf].astype(BF16), preferred_element_type=F32)
             + jnp.dot(p_o.astype(BF16), vn, preferred_element_type=F32)) / den
        o_ref[:, c0:c0 + hd] = o[:T].astype(o_ref.dtype)

    for c in copies(0, 0):
        c.start()
    for h in range(Hm):
        buf = h % 2
        if h + 1 < Hm:
            for c in copies(h + 1, 1 - buf):
                c.start()
        for c in copies(h, buf):
            c.wait()
        attend(h, buf)


def _moba_sample(proj, o_buf, q_rot, k_new, sel_flat, pt_flat, cache_k, cache_v, l, *, lay, row0, DB, T, Hm, n_pages):
    PAGE = cache_k.shape[3]
    PPB = MOBA_BLOCK // PAGE
    NS = T * MOBA_TOPK
    hd = HEAD_DIM
    MW = Hm * hd
    assert T <= 2 * SUBLANES and row0 % T == 0 and lay["MV"] % MW == 0 and cache_k.shape[2] == Hm
    grid_spec = pltpu.PrefetchScalarGridSpec(
        num_scalar_prefetch=2,
        grid=(DB,),
        in_specs=[pl.BlockSpec((T, MW), lambda b, pt, sl: (b, 0)),
                  pl.BlockSpec((T, MW), lambda b, pt, sl: (b, 0)),
                  pl.BlockSpec((T, MW), lambda b, pt, sl: (row0 // T + b, lay["MV"] // MW)),
                  pl.BlockSpec(memory_space=pl.ANY),
                  pl.BlockSpec(memory_space=pl.ANY),
                  pl.BlockSpec(memory_space=pl.ANY)],
        out_specs=pl.BlockSpec((T, MW), lambda b, pt, sl: (row0 // T + b, 0)),
        scratch_shapes=[pltpu.VMEM((2, NS * MOBA_BLOCK, hd), F32), pltpu.VMEM((2, NS * MOBA_BLOCK, hd), F32),
                        pltpu.SemaphoreType.DMA((2, 2))])
    return pl.pallas_call(
        functools.partial(_moba_sample_body, l=l, T=T, Hm=Hm, n_pages=n_pages, PPB=PPB, PAGE=PAGE),
        grid_spec=grid_spec,
        out_shape=jax.ShapeDtypeStruct(o_buf.shape, o_buf.dtype),
        input_output_aliases={7: 0},
        compiler_params=_cp(("arbitrary",)),
        name="moba_sample",
    )(pt_flat, sel_flat, q_rot, k_new, proj, cache_k, cache_v, o_buf)


def _layout(GW, MW, CC):
    def up(x, m):
        return (x + m - 1) // m * m
    lay = {"MQ": 0, "MK": MW, "MV": 2 * MW, "GQ": 3 * MW, "GK": 3 * MW + GW, "GV": 3 * MW + 2 * GW,
           "GZ": 3 * MW + 3 * GW}
    lay["BA"] = up(lay["GZ"] + GW, BA_WIDTH)
    lay["GLU"] = up(lay["BA"] + BA_WIDTH, CC)
    lay["NP"] = up(lay["GLU"] + 2 * CC, 4 * LANES)
    return lay


def _arrange_w_in(w_in, lay, GW, MW, CC, Hg):
    L, D, _ = w_in.shape
    o = 0
    qkv_a = w_in[..., o:o + 3 * GW]; o += 3 * GW
    z_a = w_in[..., o:o + GW]; o += GW
    b_a = w_in[..., o:o + Hg]; o += Hg
    a_a = w_in[..., o:o + Hg]; o += Hg
    glu = w_in[..., o:o + 2 * CC]; o += 2 * CC
    qkv_c = w_in[..., o:o + 3 * MW]; o += 3 * MW
    assert o == w_in.shape[-1]
    z = lambda n: jnp.zeros((L, D, n), w_in.dtype)
    parts = [qkv_c, qkv_a, z_a, z(lay["BA"] - lay["GZ"] - GW), b_a, z(LANES - Hg), a_a, z(LANES - Hg),
             z(lay["GLU"] - lay["BA"] - BA_WIDTH), glu, z(lay["NP"] - lay["GLU"] - 2 * CC)]
    return jnp.concatenate(parts, axis=-1).astype(BF16)


def _rope_tables(pos):
    half = HEAD_DIM // 2
    inv_freq = ROPE_THETA ** (-jnp.arange(half, dtype=F32) / half)
    ang = pos.astype(F32)[:, None] * inv_freq[None, :]
    cos, sin = jnp.cos(ang), jnp.sin(ang)
    return jnp.concatenate([cos, cos], axis=-1), jnp.concatenate([-sin, sin], axis=-1)


def _pad_lanes(a):
    L, H = a.shape
    return jnp.pad(a, ((0, 0), (0, LANES - H))).reshape(L, 1, LANES)


def kernel(x_prompt, x_sample, cache_k, cache_v, state_gdn, state_gdn_conv, state_conv, page_table, norm_mix, w_in, gdn_conv_w, gdn_a_log, gdn_dt_bias, gdn_out_norm, conv_dw_w, conv_dw_b, conv_ln_g, conv_ln_b, moba_q_norm, moba_k_norm, w_out, norm_ffn, w_gate, w_up, w_down):
    B, S, D = x_prompt.shape
    DB, T, _ = x_sample.shape
    L = w_in.shape[0]
    Hg = gdn_a_log.shape[1]
    Hm = cache_k.shape[3]
    PAGE = cache_k.shape[2]
    n_pool = cache_k.shape[1]
    n_pages = page_table.shape[1]
    CC = conv_dw_b.shape[1]
    KWG = gdn_conv_w.shape[1]
    KWC = conv_dw_w.shape[1]
    FH = w_gate.shape[2]
    GW, MW = Hg * HEAD_DIM, Hm * HEAD_DIM
    past_len = n_pages * PAGE
    assert cache_k.shape[4] == HEAD_DIM and past_len % MOBA_BLOCK == 0 and GW + CC + MW == D
    assert (past_len + T - 1) // MOBA_BLOCK == past_len // MOBA_BLOCK
    lay = _layout(GW, MW, CC)
    NP = lay["NP"]
    MP, MS = B * S, DB * T
    M = MP + MS

    w_in_r = _arrange_w_in(w_in, lay, GW, MW, CC, Hg)
    wo = w_out.astype(BF16)
    wo_a, wo_b, wo_c = wo[:, :GW], wo[:, GW:GW + CC], wo[:, GW + CC:]
    wg, wu, wd = w_gate.astype(BF16), w_up.astype(BF16), w_down.astype(BF16)
    vec3 = lambda a: a.reshape(L, 1, a.shape[-1])
    norm_mix3, norm_ffn3 = vec3(norm_mix), vec3(norm_ffn)
    og3, qg3, kg3 = vec3(gdn_out_norm), vec3(moba_q_norm), vec3(moba_k_norm)
    dwb3, lng3, lnb3 = vec3(conv_dw_b), vec3(conv_ln_g), vec3(conv_ln_b)
    alog3, dtb3 = _pad_lanes(gdn_a_log), _pad_lanes(gdn_dt_bias)
    cos_p, sin_p = _rope_tables(jnp.arange(S, dtype=I32))
    cos_s, sin_s = _rope_tables(past_len + jnp.arange(T, dtype=I32))
    pt_flat = page_table.reshape(-1).astype(I32)
    gdn_hist0 = jnp.zeros((B, KWG - 1, 3 * GW), F32)
    gdn_state0 = jnp.zeros((B, Hg, HEAD_DIM, HEAD_DIM), F32)
    conv_hist0 = jnp.zeros((B, KWC - 1, CC), F32)

    tm_in = _largest_divisor(M, 1400, SUBLANES)
    tn_in = _largest_divisor(NP, 512, LANES)
    tm_ffn = _largest_divisor(M, 700, SUBLANES)
    th_ffn = _largest_divisor(FH, 512, LANES)
    gdn_c = MOBA_BLOCK if S % MOBA_BLOCK == 0 else _largest_divisor(S, 256, SUBLANES)
    conf_t = _largest_divisor(S, 256, CONF_HDR)

    cache_kt = jnp.transpose(cache_k, (0, 1, 3, 2, 4))
    cache_vt = jnp.transpose(cache_v, (0, 1, 3, 2, 4))
    km_all = _cache_block_means(cache_kt, pt_flat, DB=DB, n_pages=n_pages)

    x = jnp.concatenate([x_prompt.reshape(MP, D), x_sample.reshape(MS, D)], axis=0)
    o_a = jnp.zeros((M, GW), BF16)
    o_b = jnp.zeros((M, CC), BF16)
    o_c = jnp.zeros((M, MW), BF16)
    kh_all = jnp.zeros((L, B, Hm, S, HEAD_DIM), F32)
    vh_all = jnp.zeros((L, B, Hm, S, HEAD_DIM), F32)
    outs = [[] for _ in range(8)]
    for l in range(L):
        proj = _inproj(x, norm_mix3, w_in_r, l, tm_in, tn_in)
        o_a, st_p = _gdn(proj, o_a, gdn_hist0, gdn_state0, gdn_conv_w, alog3, dtb3, og3, l,
                         lay=lay, row0=0, nb=B, T=S, Tt=gdn_c, C=gdn_c, HB=Hg)
        o_a, st_s = _gdn(proj, o_a, state_gdn_conv[l], state_gdn[l], gdn_conv_w, alog3, dtb3, og3, l,
                         lay=lay, row0=MP, nb=DB, T=T, Tt=T, C=LANES, HB=Hg)
        o_b, ch_p = _conformer(proj, o_b, conv_hist0, conv_dw_w, dwb3, lng3, lnb3, l,
                               lay=lay, row0=0, nb=B, T=S, Tt=conf_t)
        o_b, ch_s = _conformer(proj, o_b, state_conv[l], conv_dw_w, dwb3, lng3, lnb3, l,
                               lay=lay, row0=MP, nb=DB, T=T, Tt=T)
        o_c, kh_all, vh_all = _moba_prompt(proj, o_c, kh_all, vh_all, cos_p, sin_p, qg3, kg3, l,
                                           lay=lay, B=B, S=S, Hm=Hm)
        q_rot, k_new, sel = _moba_gate(proj, cos_s, sin_s, qg3, kg3, km_all, l,
                                       lay=lay, row0=MP, DB=DB, T=T, Hm=Hm)
        sel_flat = sel[:, :, :Hm * MOBA_TOPK].reshape(-1)
        o_c = _moba_sample(proj, o_c, q_rot, k_new, sel_flat, pt_flat, cache_kt, cache_vt, l,
                           lay=lay, row0=MP, DB=DB, T=T, Hm=Hm, n_pages=n_pages)
        x = _outproj_ffn(x, o_a, o_b, o_c, wo_a, wo_b, wo_c, norm_ffn3, wg, wu, wd, l, tm_ffn, th_ffn)

        g0, g1 = lay["GQ"], lay["GQ"] + 3 * GW
        outs[0].append(k_new.reshape(DB, T, Hm, HEAD_DIM))
        outs[1].append(proj[MP:, lay["MV"]:lay["MV"] + MW].reshape(DB, T, Hm, HEAD_DIM))
        outs[2].append(st_p)
        outs[3].append(st_s)
        outs[4].append(jnp.stack([proj[(b + 1) * S - (KWG - 1):(b + 1) * S, g0:g1] for b in range(B)]))
        hist_ext = jnp.concatenate([state_gdn_conv[l], proj[MP:, g0:g1].reshape(DB, T, 3 * GW)], axis=1)
        outs[5].append(hist_ext[:, -(KWG - 1):])
        outs[6].append(ch_p)
        outs[7].append(ch_s)

    y_prompt = x[:MP].reshape(B, S, D)
    y_sample = x[MP:].reshape(DB, T, D)
    k_prompt = jnp.transpose(kh_all, (0, 1, 3, 2, 4))
    v_prompt = jnp.transpose(vh_all, (0, 1, 3, 2, 4))
    return (y_prompt, y_sample, k_prompt, v_prompt) + tuple(jnp.stack(o) for o in outs)
```

```python
import functools

import jax
import jax.numpy as jnp
from jax import lax
from jax.experimental import pallas as pl
from jax.experimental.pallas import tpu as pltpu

F32 = jnp.float32
BF16 = jnp.bfloat16
I32 = jnp.int32

EPS = 1e-6
NEG_INF = -1e30
HEAD_DIM = 128
MOBA_BLOCK = 256
MOBA_TOPK = 3
ROPE_THETA = 10000.0
LANES = 128
SUBLANES = 8
BA_WIDTH = 2 * LANES
VMEM_LIMIT = 60 * 1024 * 1024


def _cp(sem, vmem=VMEM_LIMIT):
    return pltpu.CompilerParams(dimension_semantics=sem, vmem_limit_bytes=vmem)


def _bdot(a, b):
    return jnp.dot(a.astype(BF16), b.astype(BF16), preferred_element_type=F32)


def _nt(a, b):
    return lax.dot_general(a, b, (((1,), (1,)), ((), ())), preferred_element_type=F32)


def _bdot_nt(a, b):
    return _nt(a.astype(BF16), b.astype(BF16))


def _split_bf16(a):
    hi = a.astype(BF16)
    lo = (a - hi.astype(F32)).astype(BF16)
    return hi, lo


def _dot3_nt(a, b):
    ah, al = _split_bf16(a)
    bh, bl = _split_bf16(b)
    return _nt(ah, bh) + (_nt(ah, bl) + _nt(al, bh))


def _sigmoid(x):
    return 1.0 / (1.0 + jnp.exp(-x))


def _silu(x):
    return x * _sigmoid(x)


def _softplus(x):
    return jnp.maximum(x, 0.0) + jnp.log(1.0 + jnp.exp(-jnp.abs(x)))


def _rms(x, g):
    return x * lax.rsqrt(jnp.mean(x * x, axis=-1, keepdims=True) + EPS) * g


def _rope(x, cos_full, sin_signed):
    return x * cos_full + pltpu.roll(x, HEAD_DIM // 2, 1) * sin_signed


def _largest_divisor(n, cap, mult):
    best = None
    for d in range(mult, min(n, cap) + 1, mult):
        if n % d == 0:
            best = d
    assert best is not None, (n, cap, mult)
    return best


def _inproj_body(x_ref, g_ref, w_ref, o_ref, xn_ref):
    @pl.when(pl.program_id(1) == 0)
    def _():
        xn_ref[...] = _rms(x_ref[...], g_ref[...]).astype(BF16)

    o_ref[...] = jnp.dot(xn_ref[...], w_ref[...], preferred_element_type=F32)


def _inproj(x, g, w, l, tm, tn):
    M, D = x.shape
    NP = w.shape[-1]
    return pl.pallas_call(
        _inproj_body,
        grid=(M // tm, NP // tn),
        in_specs=[pl.BlockSpec((tm, D), lambda i, j: (i, 0)),
                  pl.BlockSpec((None, 1, D), lambda i, j: (l, 0, 0)),
                  pl.BlockSpec((None, D, tn), lambda i, j: (l, 0, j))],
        out_specs=pl.BlockSpec((tm, tn), lambda i, j: (i, j)),
        out_shape=jax.ShapeDtypeStruct((M, NP), F32),
        scratch_shapes=[pltpu.VMEM((tm, D), BF16)],
        compiler_params=_cp(("parallel", "arbitrary")),
        name="inproj",
    )(x, g, w)


def _ffn_body(x_ref, oa_ref, ob_ref, oc_ref, wa_ref, wb_ref, wc_ref, g_ref, wg_ref, wu_ref, wd_ref,
              y_ref, hn_ref):
    @pl.when(pl.program_id(1) == 0)
    def _():
        h = (x_ref[...]
             + jnp.dot(oa_ref[...], wa_ref[...], preferred_element_type=F32)
             + jnp.dot(ob_ref[...], wb_ref[...], preferred_element_type=F32)
             + jnp.dot(oc_ref[...], wc_ref[...], preferred_element_type=F32))
        y_ref[...] = h
        hn_ref[...] = _rms(h, g_ref[...]).astype(BF16)

    hn = hn_ref[...]
    a = jnp.dot(hn, wg_ref[...], preferred_element_type=F32)
    b = jnp.dot(hn, wu_ref[...], preferred_element_type=F32)
    t = (_silu(a) * b).astype(BF16)
    y_ref[...] += jnp.dot(t, wd_ref[...], preferred_element_type=F32)


def _outproj_ffn(x, oa, ob, oc, wa, wb, wc, g, wg, wu, wd, l, tm, th):
    M, D = x.shape
    FH = wg.shape[-1]
    GW, CC, MW = oa.shape[1], ob.shape[1], oc.shape[1]
    row = lambda i, j: (i, 0)
    return pl.pallas_call(
        _ffn_body,
        grid=(M // tm, FH // th),
        in_specs=[pl.BlockSpec((tm, D), row),
                  pl.BlockSpec((tm, GW), row),
                  pl.BlockSpec((tm, CC), row),
                  pl.BlockSpec((tm, MW), row),
                  pl.BlockSpec((None, GW, D), lambda i, j: (l, 0, 0), pipeline_mode=pl.Buffered(1)),
                  pl.BlockSpec((None, CC, D), lambda i, j: (l, 0, 0), pipeline_mode=pl.Buffered(1)),
                  pl.BlockSpec((None, MW, D), lambda i, j: (l, 0, 0), pipeline_mode=pl.Buffered(1)),
                  pl.BlockSpec((None, 1, D), lambda i, j: (l, 0, 0)),
                  pl.BlockSpec((None, D, th), lambda i, j: (l, 0, j)),
                  pl.BlockSpec((None, D, th), lambda i, j: (l, 0, j)),
                  pl.BlockSpec((None, th, D), lambda i, j: (l, j, 0))],
        out_specs=pl.BlockSpec((tm, D), row),
        out_shape=jax.ShapeDtypeStruct((M, D), F32),
        scratch_shapes=[pltpu.VMEM((tm, D), BF16)],
        compiler_params=_cp(("parallel", "arbitrary")),
        name="outproj_ffn",
    )(x, oa, ob, oc, wa, wb, wc, g, wg, wu, wd)


def _gdn_chunks(qs, ks, vs, betas, gs, Ss):
    nh = len(qs)
    hs = range(nh)
    C, dk = qs[0].shape
    assert C & (C - 1) == 0
    ri = lax.broadcasted_iota(I32, (C, C), 0)
    ci = lax.broadcasted_iota(I32, (C, C), 1)
    incl = ri >= ci
    eye = ri == ci
    xor = ri ^ ci
    g_row = [jnp.sum(jnp.where(eye, gs[h], 0.0), axis=0, keepdims=True) for h in hs]
    gc_row = [jnp.sum(jnp.where(ri <= ci, gs[h], 0.0), axis=0, keepdims=True) for h in hs]
    gc_col = [jnp.sum(jnp.where(incl, g_row[h], 0.0), axis=1, keepdims=True) for h in hs]
    decay = [jnp.where(incl, jnp.exp(jnp.where(incl, gc_col[h] - gc_row[h], 0.0)), 0.0) for h in hs]
    kbf = [ks[h].astype(BF16) for h in hs]
    kb = [ks[h] * betas[h] for h in hs]
    a = [jnp.where(ri > ci, _nt(kb[h].astype(BF16), kbf[h]) * decay[h], 0.0).astype(BF16) for h in hs]
    p = [jnp.where(eye, 1.0, 0.0) - jnp.where(xor == 1, a[h].astype(F32), 0.0) for h in hs]
    for lvl in range(1, C.bit_length() - 1):
        off = jnp.where((xor >> lvl) == 1, 1.0, 0.0).astype(BF16)
        pb = [p[h].astype(BF16) for h in hs]
        t1 = [jnp.dot(a[h] * off, pb[h], preferred_element_type=F32).astype(BF16) for h in hs]
        p = [p[h] - jnp.dot(pb[h], t1[h], preferred_element_type=F32) for h in hs]
    pb = [p[h].astype(BF16) for h in hs]
    egc = [jnp.exp(gc_col[h]) for h in hs]
    u = [jnp.dot(pb[h], (vs[h] * betas[h]).astype(BF16), preferred_element_type=F32) for h in hs]
    w = [jnp.dot(pb[h], (kb[h] * egc[h]).astype(BF16), preferred_element_type=F32) for h in hs]
    sb = [Ss[h].astype(BF16) for h in hs]
    v_new = [u[h] - jnp.dot(w[h].astype(BF16), sb[h], preferred_element_type=F32) for h in hs]
    vnb = [v_new[h].astype(BF16) for h in hs]
    qsc = [qs[h] * (dk ** -0.5) for h in hs]
    intra = [jnp.where(incl, _nt(qsc[h].astype(BF16), kbf[h]) * decay[h], 0.0).astype(BF16) for h in hs]
    o = [jnp.dot((qsc[h] * egc[h]).astype(BF16), sb[h], preferred_element_type=F32)
         + jnp.dot(intra[h], vnb[h], preferred_element_type=F32) for h in hs]
    g_last = [gc_col[h][C - 1:C, :] for h in hs]
    kdec_t = [(ks[h] * jnp.exp(g_last[h] - gc_col[h])).T.astype(BF16) for h in hs]
    s_new = [Ss[h] * jnp.exp(g_last[h]) + jnp.dot(kdec_t[h], vnb[h], preferred_element_type=F32) for h in hs]
    return o, s_new


def _gdn_body(q_ref, k_ref, v_ref, z_ref, ba_ref, hq_ref, hk_ref, hv_ref, s0_ref,
              wq_ref, wk_ref, wv_ref, alog_ref, dtb_ref, og_ref, obuf_hbm,
              o_ref, sout_ref, xq_ref, xk_ref, xv_ref, s_ref, *, Tt, C, HB, H, KW, NT):
    del obuf_hbm
    hb = pl.program_id(1)
    t = pl.program_id(2)
    HR = SUBLANES
    xrefs = (xq_ref, xk_ref, xv_ref)

    @pl.when(t == 0)
    def _():
        for xr, hr in zip(xrefs, (hq_ref, hk_ref, hv_ref)):
            xr[HR - (KW - 1):HR, :] = hr[...]
        s_ref[...] = s0_ref[...]

    for xr, r in zip(xrefs, (q_ref, k_ref, v_ref)):
        xr[HR:HR + Tt, :] = r[...]
        if C > Tt:
            xr[HR + Tt:HR + C, :] = jnp.zeros((C - Tt, xr.shape[1]), F32)

    def conv(xr, w_ref, s):
        acc = None
        for j in range(KW):
            term = xr[HR - (KW - 1) + j:HR - (KW - 1) + j + C, s * LANES:(s + 1) * LANES] \
                * w_ref[j:j + 1, s * LANES:(s + 1) * LANES]
            acc = term if acc is None else acc + term
        return acc

    ba = ba_ref[...]
    if C > Tt:
        ba = jnp.concatenate([ba, jnp.zeros((C - Tt, ba.shape[1]), F32)], axis=0)
    beta_all = _sigmoid(ba[:, :LANES])
    g_all = -jnp.exp(alog_ref[...]) * _softplus(ba[:, LANES:] + dtb_ref[...])
    lane = lax.broadcasted_iota(I32, (1, LANES), 1)
    rvalid = lax.broadcasted_iota(I32, (C, 1), 0) < Tt

    qs, ks, vs, betas, gs = [], [], [], [], []
    for s in range(HB):
        hsel = lane == hb * HB + s
        beta = jnp.sum(jnp.where(hsel, beta_all, 0.0), axis=1, keepdims=True)
        g = jnp.sum(jnp.where(hsel, g_all, 0.0), axis=1, keepdims=True)
        q = _silu(conv(xq_ref, wq_ref, s))
        k = _silu(conv(xk_ref, wk_ref, s))
        v = _silu(conv(xv_ref, wv_ref, s))
        q = q * lax.rsqrt(jnp.sum(q * q, axis=-1, keepdims=True) + EPS)
        k = k * lax.rsqrt(jnp.sum(k * k, axis=-1, keepdims=True) + EPS)
        if C > Tt:
            q = jnp.where(rvalid, q, 0.0)
            k = jnp.where(rvalid, k, 0.0)
            v = jnp.where(rvalid, v, 0.0)
            beta = jnp.where(rvalid, beta, 0.0)
            g = jnp.where(rvalid, g, 0.0)
        for lst, val in zip((qs, ks, vs, betas, gs), (q, k, v, beta, g)):
            lst.append(val)
    os_, s_new = _gdn_chunks(qs, ks, vs, betas, gs, [s_ref[s] for s in range(HB)])
    for s in range(HB):
        s_ref[s] = s_new[s]
        zz = z_ref[:, s * LANES:(s + 1) * LANES]
        o_ref[:, s * LANES:(s + 1) * LANES] = (_rms(os_[s][:Tt], og_ref[...]) * _silu(zz)).astype(o_ref.dtype)

    if NT > 1:
        for xr in xrefs:
            xr[0:HR, :] = xr[C:C + HR, :]

    @pl.when(t == NT - 1)
    def _():
        sout_ref[...] = s_ref[...]


def _gdn(proj, o_buf, hist, s0, conv_w, alog, dtb, og, l, *, lay, row0, nb, T, Tt, C, HB):
    M = proj.shape[0]
    H = s0.shape[1]
    GW = H * HEAD_DIM
    KW = conv_w.shape[1]
    NT = T // Tt
    W = HB * LANES
    assert T % Tt == 0 and row0 % Tt == 0 and H % HB == 0 and (NT == 1 or Tt == C)
    rb0 = row0 // Tt

    def col(off):
        assert off % W == 0
        return lambda b, h, t: (rb0 + b * NT + t, off // W + h)

    def hcol(off):
        return lambda b, h, t: (b, 0, off // W + h)

    def wcol(off):
        return lambda b, h, t: (l, 0, off // W + h)

    vec = lambda b, h, t: (l, 0, 0)
    in_specs = [pl.BlockSpec((Tt, W), col(lay["GQ"])),
                pl.BlockSpec((Tt, W), col(lay["GK"])),
                pl.BlockSpec((Tt, W), col(lay["GV"])),
                pl.BlockSpec((Tt, W), col(lay["GZ"])),
                pl.BlockSpec((Tt, BA_WIDTH), lambda b, h, t: (rb0 + b * NT + t, lay["BA"] // BA_WIDTH)),
                pl.BlockSpec((None, KW - 1, W), hcol(0)),
                pl.BlockSpec((None, KW - 1, W), hcol(GW)),
                pl.BlockSpec((None, KW - 1, W), hcol(2 * GW)),
                pl.BlockSpec((None, HB, HEAD_DIM, HEAD_DIM), lambda b, h, t: (b, h, 0, 0)),
                pl.BlockSpec((None, KW, W), wcol(0)),
                pl.BlockSpec((None, KW, W), wcol(GW)),
                pl.BlockSpec((None, KW, W), wcol(2 * GW)),
                pl.BlockSpec((None, 1, LANES), vec),
                pl.BlockSpec((None, 1, LANES), vec),
                pl.BlockSpec((None, 1, HEAD_DIM), vec),
                pl.BlockSpec(memory_space=pl.ANY)]
    args = [proj, proj, proj, proj, proj, hist, hist, hist, s0, conv_w, conv_w, conv_w, alog, dtb, og, o_buf]
    assert o_buf.shape == (M, GW) and o_buf.dtype == BF16
    o, st = pl.pallas_call(
        functools.partial(_gdn_body, Tt=Tt, C=C, HB=HB, H=H, KW=KW, NT=NT),
        grid=(nb, H // HB, NT),
        in_specs=in_specs,
        out_specs=[pl.BlockSpec((Tt, W), lambda b, h, t: (rb0 + b * NT + t, h)),
                   pl.BlockSpec((None, HB, HEAD_DIM, HEAD_DIM), lambda b, h, t: (b, h, 0, 0))],
        out_shape=[jax.ShapeDtypeStruct((M, GW), BF16),
                   jax.ShapeDtypeStruct((nb, H, HEAD_DIM, HEAD_DIM), F32)],
        scratch_shapes=[pltpu.VMEM((SUBLANES + C, W), F32)] * 3 + [pltpu.VMEM((HB, HEAD_DIM, HEAD_DIM), F32)],
        input_output_aliases={len(args) - 1: 0},
        compiler_params=_cp(("parallel", "parallel", "arbitrary")),
        name="gdn",
    )(*args)
    return o, st


CONF_HDR = 32
CONF_SUB = 32


def _conf_body(a_ref, gt_ref, hist_ref, w_ref, b_ref, lg_ref, lb_ref, obuf_hbm, o_ref, hout_ref, buf_ref, sh_ref,
               *, Tt, KW, NT):
    del obuf_hbm
    t = pl.program_id(1)
    HR = CONF_HDR
    base = HR - (KW - 1)

    @pl.when(t == 0)
    def _():
        buf_ref[0:HR, :] = jnp.zeros((HR, buf_ref.shape[1]), F32)
        buf_ref[base:HR, :] = hist_ref[...]

    buf_ref[HR:HR + Tt, :] = a_ref[...] * _sigmoid(gt_ref[...])
    span = HR + Tt - SUBLANES
    for r in range(1, SUBLANES):
        sh_ref[r - 1, 0:span, :] = buf_ref[r:r + span, :]

    def tap_rows(j, r0, n):
        a, r = divmod(base + j, SUBLANES)
        lo = r0 + a * SUBLANES
        return buf_ref[lo:lo + n, :] if r == 0 else sh_ref[r - 1, lo:lo + n, :]

    sub = min(CONF_SUB, Tt)
    for r0 in range(0, Tt, sub):
        acc = None
        for j in range(KW):
            term = tap_rows(j, r0, sub) * w_ref[j:j + 1, :]
            acc = term if acc is None else acc + term
        y = acc + b_ref[...]
        yc = y - jnp.mean(y, axis=-1, keepdims=True)
        yn = yc * lax.rsqrt(jnp.mean(yc * yc, axis=-1, keepdims=True) + EPS) * lg_ref[...] + lb_ref[...]
        o_ref[r0:r0 + sub, :] = _silu(yn).astype(o_ref.dtype)

    @pl.when(t == NT - 1)
    def _():
        hout_ref[...] = buf_ref[Tt + base:Tt + HR, :]

    if NT > 1:
        buf_ref[0:HR, :] = buf_ref[Tt:Tt + HR, :]


def _conformer(proj, o_buf, hist, dw_w, dw_b, ln_g, ln_b, l, *, lay, row0, nb, T, Tt):
    M = proj.shape[0]
    CC = dw_w.shape[2]
    KW = dw_w.shape[1]
    NT = T // Tt
    assert T % Tt == 0 and row0 % Tt == 0 and lay["GLU"] % CC == 0 and KW - 1 <= CONF_HDR
    assert NT == 1 or Tt >= CONF_HDR
    rb0 = row0 // Tt
    cb = lay["GLU"] // CC
    vec = lambda b, t: (l, 0, 0)
    in_specs = [pl.BlockSpec((Tt, CC), lambda b, t: (rb0 + b * NT + t, cb)),
                pl.BlockSpec((Tt, CC), lambda b, t: (rb0 + b * NT + t, cb + 1)),
                pl.BlockSpec((None, KW - 1, CC), lambda b, t: (b, 0, 0)),
                pl.BlockSpec((None, KW, CC), vec),
                pl.BlockSpec((None, 1, CC), vec),
                pl.BlockSpec((None, 1, CC), vec),
                pl.BlockSpec((None, 1, CC), vec),
                pl.BlockSpec(memory_space=pl.ANY)]
    args = [proj, proj, hist, dw_w, dw_b, ln_g, ln_b, o_buf]
    assert o_buf.shape == (M, CC) and o_buf.dtype == BF16
    o, hn = pl.pallas_call(
        functools.partial(_conf_body, Tt=Tt, KW=KW, NT=NT),
        grid=(nb, NT),
        in_specs=in_specs,
        out_specs=[pl.BlockSpec((Tt, CC), lambda b, t: (rb0 + b * NT + t, 0)),
                   pl.BlockSpec((None, KW - 1, CC), lambda b, t: (b, 0, 0))],
        out_shape=[jax.ShapeDtypeStruct((M, CC), BF16),
                   jax.ShapeDtypeStruct((nb, KW - 1, CC), F32)],
        scratch_shapes=[pltpu.VMEM((CONF_HDR + Tt, CC), F32),
                        pltpu.VMEM((SUBLANES - 1, CONF_HDR + Tt, CC), F32)],
        input_output_aliases={len(args) - 1: 0},
        compiler_params=_cp(("parallel", "arbitrary")),
        name="conformer",
    )(*args)
    return o, hn


def _moba_prompt_body(q_ref, k_ref, v_ref, cq_ref, sq_ref, ck_ref, sk_ref, qg_ref, kg_ref,
                      obuf_hbm, khbuf_hbm, vhbuf_hbm,
                      o_ref, kh_ref, vh_ref, kbf_ref, vt_ref, km_ref, sel_ref, *, NB, HPS):
    del obuf_hbm, khbuf_hbm, vhbuf_hbm
    i = pl.program_id(2)
    BK = MOBA_BLOCK
    hd = HEAD_DIM
    heads = range(HPS)

    @pl.when(i == 0)
    def _():
        def blk(n, carry):
            r = pl.multiple_of(n * BK, BK)
            cos_b, sin_b = ck_ref[pl.ds(r, BK), :], sk_ref[pl.ds(r, BK), :]
            for hh in heads:
                cs = slice(hh * hd, (hh + 1) * hd)
                kr = _rope(_rms(k_ref[pl.ds(r, BK), cs], kg_ref[...]), cos_b, sin_b)
                vb = v_ref[pl.ds(r, BK), cs]
                kh_ref[hh, pl.ds(r, BK), :] = kr
                vh_ref[hh, pl.ds(r, BK), :] = vb
                kbf_ref[hh, n] = kr.astype(BF16)
                vt_ref[hh, n] = vb.T.astype(BF16)
                km_ref[hh, pl.ds(n, 1), :] = jnp.sum(kr, axis=0, keepdims=True) * (1.0 / BK)
            return carry
        lax.fori_loop(0, NB, blk, 0)

    qs = []
    for hh in heads:
        qr = _rope(_rms(q_ref[:, hh * hd:(hh + 1) * hd], qg_ref[...]), cq_ref[...], sq_ref[...])
        gate = _dot3_nt(km_ref[hh], qr)
        nidx = lax.broadcasted_iota(I32, gate.shape, 0)
        past = nidx < i
        gm = jnp.where(past, gate, NEG_INF)
        rank = jnp.zeros(gate.shape, I32)
        for m in range(NB):
            grow = gm[m:m + 1, :]
            beats = (grow > gm) | ((grow == gm) & (nidx > m))
            rank = rank + jnp.where(beats, 1, 0)
        sel_ref[hh] = jnp.where(past & (rank < MOBA_TOPK), 1.0, 0.0)
        qs.append((qr * (hd ** -0.5)).astype(BF16))

    def update(carry, s_list, vt_list):
        m_prev, l_prev, acc = carry
        mx = None
        for s in s_list:
            cur = jnp.max(s, axis=0, keepdims=True)
            mx = cur if mx is None else jnp.maximum(mx, cur)
        m_new = jnp.maximum(m_prev, mx)
        alpha = jnp.exp(m_prev - m_new)
        l_new = alpha * l_prev
        acc = alpha * acc
        for s, vt in zip(s_list, vt_list):
            p = jnp.exp(s - m_new)
            l_new = l_new + jnp.sum(p, axis=0, keepdims=True)
            acc = acc + jnp.dot(vt, p.astype(BF16), preferred_element_type=F32)
        return m_new, l_new, acc

    def update_all(carries, s_all, vt_all):
        m_new, alpha, p_all = [], [], []
        for hh in heads:
            mx = None
            for s in s_all[hh]:
                cur = jnp.max(s, axis=0, keepdims=True)
                mx = cur if mx is None else jnp.maximum(mx, cur)
            m_new.append(jnp.maximum(carries[hh][0], mx))
            alpha.append(jnp.exp(carries[hh][0] - m_new[hh]))
        for hh in heads:
            p_all.append([jnp.exp(s - m_new[hh]) for s in s_all[hh]])
        out = []
        for hh in heads:
            l_new = alpha[hh] * carries[hh][1]
            acc = alpha[hh] * carries[hh][2]
            for p, vt in zip(p_all[hh], vt_all[hh]):
                l_new = l_new + jnp.sum(p, axis=0, keepdims=True)
                acc = acc + jnp.dot(vt, p.astype(BF16), preferred_element_type=F32)
            out.append((m_new[hh], l_new, acc))
        return tuple(out)

    def group(n0, width, carries):
        s_all = [[jnp.where(sel_ref[hh, pl.ds(n0 + d, 1), :] > 0.5,
                            _nt(kbf_ref[hh, n0 + d], qs[hh]), NEG_INF)
                  for d in range(width)] for hh in heads]
        return update_all(carries, s_all, [[vt_ref[hh, n0 + d] for d in range(width)] for hh in heads])

    init = tuple((jnp.full((1, BK), NEG_INF, F32), jnp.zeros((1, BK), F32), jnp.zeros((hd, BK), F32))
                 for _ in heads)
    gw = 4 // HPS if HPS <= 2 else 1
    nwide = i // gw if gw > 1 else 0
    carries = init
    if gw > 1:
        carries = lax.fori_loop(0, nwide, lambda g, c: group(gw * g, gw, c), carries)
    rem = i - gw * nwide if gw > 1 else i
    if gw == 4:
        carries = lax.fori_loop(0, (rem + 1) // 2, lambda g, c: group(gw * nwide + 2 * g, 2, c), carries)
    else:
        carries = lax.fori_loop(0, rem, lambda g, c: group(gw * nwide + g, 1, c), carries)
    kpos = lax.broadcasted_iota(I32, (BK, BK), 0)
    qpos = lax.broadcasted_iota(I32, (BK, BK), 1)
    s_own = [jnp.where(kpos <= qpos, _nt(kbf_ref[hh, i], qs[hh]), NEG_INF) for hh in heads]
    for hh in heads:
        _, l_fin, acc = update(carries[hh], [s_own[hh]], [vt_ref[hh, i]])
        o_ref[:, hh * hd:(hh + 1) * hd] = (acc / l_fin).T.astype(o_ref.dtype)


def _moba_prompt(proj, o_buf, kh_buf, vh_buf, cos_t, sin_t, qg, kg, l, *, lay, B, S, Hm):
    M = proj.shape[0]
    BK = MOBA_BLOCK
    assert S % BK == 0
    NB = S // BK
    MW = Hm * HEAD_DIM
    hd = HEAD_DIM
    vec = lambda b, h, i: (l, 0, 0)
    L = kh_buf.shape[0]
    assert kh_buf.shape == vh_buf.shape == (L, B, Hm, S, hd) and kh_buf.dtype == vh_buf.dtype == F32
    HPS = 2 if Hm % 2 == 0 else 1
    W = HPS * hd
    assert lay["MQ"] % W == 0 and lay["MK"] % W == 0 and lay["MV"] % W == 0
    kv_out = pl.BlockSpec((None, None, HPS, S, hd), lambda b, h, i: (l, b, h, 0, 0))
    return pl.pallas_call(
        functools.partial(_moba_prompt_body, NB=NB, HPS=HPS),
        grid=(B, Hm // HPS, NB),
        in_specs=[pl.BlockSpec((BK, W), lambda b, h, i: (b * NB + i, lay["MQ"] // W + h)),
                  pl.BlockSpec((S, W), lambda b, h, i: (b, lay["MK"] // W + h)),
                  pl.BlockSpec((S, W), lambda b, h, i: (b, lay["MV"] // W + h)),
                  pl.BlockSpec((BK, hd), lambda b, h, i: (i, 0)),
                  pl.BlockSpec((BK, hd), lambda b, h, i: (i, 0)),
                  pl.BlockSpec((S, hd), lambda b, h, i: (0, 0)),
                  pl.BlockSpec((S, hd), lambda b, h, i: (0, 0)),
                  pl.BlockSpec((None, 1, hd), vec),
                  pl.BlockSpec((None, 1, hd), vec),
                  pl.BlockSpec(memory_space=pl.ANY),
                  pl.BlockSpec(memory_space=pl.ANY),
                  pl.BlockSpec(memory_space=pl.ANY)],
        out_specs=[pl.BlockSpec((BK, W), lambda b, h, i: (b * NB + i, h)), kv_out, kv_out],
        out_shape=[jax.ShapeDtypeStruct((M, MW), BF16),
                   jax.ShapeDtypeStruct(kh_buf.shape, F32),
                   jax.ShapeDtypeStruct(vh_buf.shape, F32)],
        scratch_shapes=[pltpu.VMEM((HPS, NB, BK, hd), BF16), pltpu.VMEM((HPS, NB, hd, BK), BF16),
                        pltpu.VMEM((HPS, NB, hd), F32), pltpu.VMEM((HPS, NB, BK), F32)],
        input_output_aliases={9: 0, 10: 1, 11: 2},
        compiler_params=_cp(("parallel", "parallel", "arbitrary")),
        name="moba_prompt",
    )(proj, proj, proj, cos_t, sin_t, cos_t, sin_t, qg, kg, o_buf, kh_buf, vh_buf)


KM_GROUP = 8


def _kmeans_body(pt_ref, *refs, PPB, PAGE):
    ins, o_ref = refs[:-1], refs[-1]
    for g in range(KM_GROUP):
        s = None
        for p in range(PPB):
            part = jnp.sum(ins[g * PPB + p][...], axis=1)
            s = part if s is None else s + part
        o_ref[g] = s * (1.0 / (PPB * PAGE))


def _cache_block_means(cache_k, pt_flat, *, DB, n_pages):
    L, _, Hm, PAGE, hd = cache_k.shape
    PPB = MOBA_BLOCK // PAGE
    NBK = n_pages // PPB
    assert MOBA_BLOCK % PAGE == 0 and n_pages % PPB == 0 and NBK % KM_GROUP == 0
    PG = KM_GROUP * PPB

    def page_spec(p):
        return pl.BlockSpec((None, None, Hm, PAGE, hd),
                            lambda l, b, g, pt: (l, pt[b * n_pages + g * PG + p], 0, 0, 0))

    return pl.pallas_call(
        functools.partial(_kmeans_body, PPB=PPB, PAGE=PAGE),
        grid_spec=pltpu.PrefetchScalarGridSpec(
            num_scalar_prefetch=1,
            grid=(L, DB, NBK // KM_GROUP),
            in_specs=[page_spec(p) for p in range(PG)],
            out_specs=pl.BlockSpec((None, None, KM_GROUP, Hm, hd), lambda l, b, g, pt: (l, b, g, 0, 0))),
        out_shape=jax.ShapeDtypeStruct((L, DB, NBK, Hm, hd), F32),
        compiler_params=_cp(("parallel", "parallel", "arbitrary")),
        name="cache_block_means",
    )(pt_flat, *([cache_k] * PG))


def _moba_gate_body(qkv_ref, cos_ref, sin_ref, qg_ref, kg_ref, km_ref, q_ref, k_ref, sel_ref, *, Hm, NBK):
    MW = Hm * HEAD_DIM
    T = qkv_ref.shape[0]
    lane_o = lax.broadcasted_iota(I32, (T, LANES), 1)
    lane_g = lax.broadcasted_iota(I32, (T, NBK), 1)
    sel = jnp.zeros((T, LANES), I32)
    for h in range(Hm):
        c0 = h * HEAD_DIM
        qr = _rope(_rms(qkv_ref[:, c0:c0 + HEAD_DIM], qg_ref[...]), cos_ref[...], sin_ref[...])
        kr = _rope(_rms(qkv_ref[:, MW + c0:MW + c0 + HEAD_DIM], kg_ref[...]), cos_ref[...], sin_ref[...])
        q_ref[:, c0:c0 + HEAD_DIM] = qr
        k_ref[:, c0:c0 + HEAD_DIM] = kr
        gate = _dot3_nt(qr, km_ref[:, h, :])
        for j in range(MOBA_TOPK):
            mx = jnp.max(gate, axis=1, keepdims=True)
            idx = jnp.min(jnp.where(gate == mx, lane_g, NBK), axis=1, keepdims=True)
            sel = jnp.where(lane_o == h * MOBA_TOPK + j, idx, sel)
            gate = jnp.where(lane_g == idx, -jnp.inf, gate)
    sel_ref[...] = sel


def _moba_gate(proj, cos_s, sin_s, qg, kg, km, l, *, lay, row0, DB, T, Hm):
    MW = Hm * HEAD_DIM
    NBK = km.shape[2]
    assert row0 % T == 0 and lay["MQ"] == 0 and lay["MK"] == MW and Hm * MOBA_TOPK <= LANES and NBK >= MOBA_TOPK
    vec = lambda b: (l, 0, 0)
    return pl.pallas_call(
        functools.partial(_moba_gate_body, Hm=Hm, NBK=NBK),
        grid=(DB,),
        in_specs=[pl.BlockSpec((T, 3 * MW), lambda b: (row0 // T + b, 0)),
                  pl.BlockSpec((T, HEAD_DIM), lambda b: (0, 0)),
                  pl.BlockSpec((T, HEAD_DIM), lambda b: (0, 0)),
                  pl.BlockSpec((None, 1, HEAD_DIM), vec),
                  pl.BlockSpec((None, 1, HEAD_DIM), vec),
                  pl.BlockSpec((None, None, NBK, Hm, HEAD_DIM), lambda b: (l, b, 0, 0, 0))],
        out_specs=[pl.BlockSpec((T, MW), lambda b: (b, 0)),
                   pl.BlockSpec((T, MW), lambda b: (b, 0)),
                   pl.BlockSpec((None, T, LANES), lambda b: (b, 0, 0))],
        out_shape=[jax.ShapeDtypeStruct((DB * T, MW), F32),
                   jax.ShapeDtypeStruct((DB * T, MW), F32),
                   jax.ShapeDtypeStruct((DB, T, LANES), I32)],
        compiler_params=_cp(("parallel",)),
        name="moba_gate",
    )(proj, cos_s, sin_s, qg, kg, km)


def _moba_sample_body(pt_ref, sel_ref, q_ref, k_ref, v_ref, ck_hbm, cv_hbm, obuf_hbm, o_ref,
                      kbuf_ref, vbuf_ref, sem_ref, *, l, T, Hm, n_pages, PPB, PAGE):
    del obuf_hbm
    b = pl.program_id(0)
    BK = MOBA_BLOCK
    NS = T * MOBA_TOPK
    hd = HEAD_DIM

    def copies(h, buf):
        out = []
        for slot in range(NS):
            t, j = divmod(slot, MOBA_TOPK)
            blk = sel_ref[((b * T + t) * Hm + h) * MOBA_TOPK + j]
            for p in range(PPB):
                page = pt_ref[b * n_pages + blk * PPB + p]
                dst = pl.ds(slot * BK + p * PAGE, PAGE)
                out.append(pltpu.make_async_copy(ck_hbm.at[l, page, h], kbuf_ref.at[buf, dst, :], sem_ref.at[0, buf]))
                out.append(pltpu.make_async_copy(cv_hbm.at[l, page, h], vbuf_ref.at[buf, dst, :], sem_ref.at[1, buf]))
        return out

    TP = 2 * SUBLANES
    zpad = jnp.zeros((TP - T, hd), F32)
    span = MOBA_TOPK * BK

    def attend(h, buf):
        c0 = h * hd
        qs = jnp.concatenate([q_ref[:, c0:c0 + hd] * (hd ** -0.5), zpad], axis=0).astype(BF16)
        kn = jnp.concatenate([k_ref[:, c0:c0 + hd], zpad], axis=0).astype(BF16)
        vn = jnp.concatenate([v_ref[:, c0:c0 + hd], zpad], axis=0).astype(BF16)
        s_p = _nt(qs, kbuf_ref[buf].astype(BF16))
        ri = lax.broadcasted_iota(I32, s_p.shape, 0)
        ci = lax.broadcasted_iota(I32, s_p.shape, 1)
        s_p = jnp.where((ci >= ri * span) & (ci < (ri + 1) * span), s_p, NEG_INF)
        s_o = _nt(qs, kn)
        ro = lax.broadcasted_iota(I32, s_o.shape, 0)
        co = lax.broadcasted_iota(I32, s_o.shape, 1)
        s_o = jnp.where((co <= ro) & (co < T), s_o, NEG_INF)
        m = jnp.maximum(jnp.max(s_p, axis=1, keepdims=True), jnp.max(s_o, axis=1, keepdims=True))
        p_p = jnp.exp(s_p - m)
        p_o = jnp.exp(s_o - m)
        den = jnp.sum(p_p, axis=1, keepdims=True) + jnp.sum(p_o, axis=1, keepdims=True)
        o = (jnp.dot(p_p.astype(BF16), vbuf_ref[buf].astype(BF16), preferred_element_type=F32)
             + jnp.dot(p_o.astype(BF16), vn, preferred_element_type=F32)) / den
        o_ref[:, c0:c0 + hd] = o[:T].astype(o_ref.dtype)

    for c in copies(0, 0):
        c.start()
    for h in range(Hm):
        buf = h % 2
        if h + 1 < Hm:
            for c in copies(h + 1, 1 - buf):
                c.start()
        for c in copies(h, buf):
            c.wait()
        attend(h, buf)


def _moba_sample(proj, o_buf, q_rot, k_new, sel_flat, pt_flat, cache_k, cache_v, l, *, lay, row0, DB, T, Hm, n_pages):
    PAGE = cache_k.shape[3]
    PPB = MOBA_BLOCK // PAGE
    NS = T * MOBA_TOPK
    hd = HEAD_DIM
    MW = Hm * hd
    assert T <= 2 * SUBLANES and row0 % T == 0 and lay["MV"] % MW == 0 and cache_k.shape[2] == Hm
    grid_spec = pltpu.PrefetchScalarGridSpec(
        num_scalar_prefetch=2,
        grid=(DB,),
        in_specs=[pl.BlockSpec((T, MW), lambda b, pt, sl: (b, 0)),
                  pl.BlockSpec((T, MW), lambda b, pt, sl: (b, 0)),
                  pl.BlockSpec((T, MW), lambda b, pt, sl: (row0 // T + b, lay["MV"] // MW)),
                  pl.BlockSpec(memory_space=pl.ANY),
                  pl.BlockSpec(memory_space=pl.ANY),
                  pl.BlockSpec(memory_space=pl.ANY)],
        out_specs=pl.BlockSpec((T, MW), lambda b, pt, sl: (row0 // T + b, 0)),
        scratch_shapes=[pltpu.VMEM((2, NS * MOBA_BLOCK, hd), F32), pltpu.VMEM((2, NS * MOBA_BLOCK, hd), F32),
                        pltpu.SemaphoreType.DMA((2, 2))])
    return pl.pallas_call(
        functools.partial(_moba_sample_body, l=l, T=T, Hm=Hm, n_pages=n_pages, PPB=PPB, PAGE=PAGE),
        grid_spec=grid_spec,
        out_shape=jax.ShapeDtypeStruct(o_buf.shape, o_buf.dtype),
        input_output_aliases={7: 0},
        compiler_params=_cp(("arbitrary",)),
        name="moba_sample",
    )(pt_flat, sel_flat, q_rot, k_new, proj, cache_k, cache_v, o_buf)


def _layout(GW, MW, CC):
    def up(x, m):
        return (x + m - 1) // m * m
    lay = {"MQ": 0, "MK": MW, "MV": 2 * MW, "GQ": 3 * MW, "GK": 3 * MW + GW, "GV": 3 * MW + 2 * GW,
           "GZ": 3 * MW + 3 * GW}
    lay["BA"] = up(lay["GZ"] + GW, BA_WIDTH)
    lay["GLU"] = up(lay["BA"] + BA_WIDTH, CC)
    lay["NP"] = up(lay["GLU"] + 2 * CC, 4 * LANES)
    return lay


def _arrange_w_in(w_in, lay, GW, MW, CC, Hg):
    L, D, _ = w_in.shape
    o = 0
    qkv_a = w_in[..., o:o + 3 * GW]; o += 3 * GW
    z_a = w_in[..., o:o + GW]; o += GW
    b_a = w_in[..., o:o + Hg]; o += Hg
    a_a = w_in[..., o:o + Hg]; o += Hg
    glu = w_in[..., o:o + 2 * CC]; o += 2 * CC
    qkv_c = w_in[..., o:o + 3 * MW]; o += 3 * MW
    assert o == w_in.shape[-1]
    z = lambda n: jnp.zeros((L, D, n), w_in.dtype)
    parts = [qkv_c, qkv_a, z_a, z(lay["BA"] - lay["GZ"] - GW), b_a, z(LANES - Hg), a_a, z(LANES - Hg),
             z(lay["GLU"] - lay["BA"] - BA_WIDTH), glu, z(lay["NP"] - lay["GLU"] - 2 * CC)]
    return jnp.concatenate(parts, axis=-1).astype(BF16)


def _rope_tables(pos):
    half = HEAD_DIM // 2
    inv_freq = ROPE_THETA ** (-jnp.arange(half, dtype=F32) / half)
    ang = pos.astype(F32)[:, None] * inv_freq[None, :]
    cos, sin = jnp.cos(ang), jnp.sin(ang)
    return jnp.concatenate([cos, cos], axis=-1), jnp.concatenate([-sin, sin], axis=-1)


def _pad_lanes(a):
    L, H = a.shape
    return jnp.pad(a, ((0, 0), (0, LANES - H))).reshape(L, 1, LANES)


def kernel(x_prompt, x_sample, cache_k, cache_v, state_gdn, state_gdn_conv, state_conv, page_table, norm_mix, w_in, gdn_conv_w, gdn_a_log, gdn_dt_bias, gdn_out_norm, conv_dw_w, conv_dw_b, conv_ln_g, conv_ln_b, moba_q_norm, moba_k_norm, w_out, norm_ffn, w_gate, w_up, w_down):
    B, S, D = x_prompt.shape
    DB, T, _ = x_sample.shape
    L = w_in.shape[0]
    Hg = gdn_a_log.shape[1]
    Hm = cache_k.shape[3]
    PAGE = cache_k.shape[2]
    n_pool = cache_k.shape[1]
    n_pages = page_table.shape[1]
    CC = conv_dw_b.shape[1]
    KWG = gdn_conv_w.shape[1]
    KWC = conv_dw_w.shape[1]
    FH = w_gate.shape[2]
    GW, MW = Hg * HEAD_DIM, Hm * HEAD_DIM
    past_len = n_pages * PAGE
    assert cache_k.shape[4] == HEAD_DIM and past_len % MOBA_BLOCK == 0 and GW + CC + MW == D
    assert (past_len + T - 1) // MOBA_BLOCK == past_len // MOBA_BLOCK
    lay = _layout(GW, MW, CC)
    NP = lay["NP"]
    MP, MS = B * S, DB * T
    M = MP + MS

    w_in_r = _arrange_w_in(w_in, lay, GW, MW, CC, Hg)
    wo = w_out.astype(BF16)
    wo_a, wo_b, wo_c = wo[:, :GW], wo[:, GW:GW + CC], wo[:, GW + CC:]
    wg, wu, wd = w_gate.astype(BF16), w_up.astype(BF16), w_down.astype(BF16)
    vec3 = lambda a: a.reshape(L, 1, a.shape[-1])
    norm_mix3, norm_ffn3 = vec3(norm_mix), vec3(norm_ffn)
    og3, qg3, kg3 = vec3(gdn_out_norm), vec3(moba_q_norm), vec3(moba_k_norm)
    dwb3, lng3, lnb3 = vec3(conv_dw_b), vec3(conv_ln_g), vec3(conv_ln_b)
    alog3, dtb3 = _pad_lanes(gdn_a_log), _pad_lanes(gdn_dt_bias)
    cos_p, sin_p = _rope_tables(jnp.arange(S, dtype=I32))
    cos_s, sin_s = _rope_tables(past_len + jnp.arange(T, dtype=I32))
    pt_flat = page_table.reshape(-1).astype(I32)
    gdn_hist0 = jnp.zeros((B, KWG - 1, 3 * GW), F32)
    gdn_state0 = jnp.zeros((B, Hg, HEAD_DIM, HEAD_DIM), F32)
    conv_hist0 = jnp.zeros((B, KWC - 1, CC), F32)

    tm_in = _largest_divisor(M, 1400, SUBLANES)
    tn_in = _largest_divisor(NP, 512, LANES)
    tm_ffn = _largest_divisor(M, 700, SUBLANES)
    th_ffn = _largest_divisor(FH, 512, LANES)
    gdn_c = MOBA_BLOCK if S % MOBA_BLOCK == 0 else _largest_divisor(S, 256, SUBLANES)
    conf_t = _largest_divisor(S, 256, CONF_HDR)

    cache_kt = jnp.transpose(cache_k, (0, 1, 3, 2, 4))
    cache_vt = jnp.transpose(cache_v, (0, 1, 3, 2, 4))
    km_all = _cache_block_means(cache_kt, pt_flat, DB=DB, n_pages=n_pages)

    x = jnp.concatenate([x_prompt.reshape(MP, D), x_sample.reshape(MS, D)], axis=0)
    o_a = jnp.zeros((M, GW), BF16)
    o_b = jnp.zeros((M, CC), BF16)
    o_c = jnp.zeros((M, MW), BF16)
    kh_all = jnp.zeros((L, B, Hm, S, HEAD_DIM), F32)
    vh_all = jnp.zeros((L, B, Hm, S, HEAD_DIM), F32)
    outs = [[] for _ in range(8)]
    for l in range(L):
        proj = _inproj(x, norm_mix3, w_in_r, l, tm_in, tn_in)
        o_a, st_p = _gdn(proj, o_a, gdn_hist0, gdn_state0, gdn_conv_w, alog3, dtb3, og3, l,
                         lay=lay, row0=0, nb=B, T=S, Tt=gdn_c, C=gdn_c, HB=Hg)
        o_a, st_s = _gdn(proj, o_a, state_gdn_conv[l], state_gdn[l], gdn_conv_w, alog3, dtb3, og3, l,
                         lay=lay, row0=MP, nb=DB, T=T, Tt=T, C=LANES, HB=Hg)
        o_b, ch_p = _conformer(proj, o_b, conv_hist0, conv_dw_w, dwb3, lng3, lnb3, l,
                               lay=lay, row0=0, nb=B, T=S, Tt=conf_t)
        o_b, ch_s = _conformer(proj, o_b, state_conv[l], conv_dw_w, dwb3, lng3, lnb3, l,
                               lay=lay, row0=MP, nb=DB, T=T, Tt=T)
        o_c, kh_all, vh_all = _moba_prompt(proj, o_c, kh_all, vh_all, cos_p, sin_p, qg3, kg3, l,
                                           lay=lay, B=B, S=S, Hm=Hm)
        q_rot, k_new, sel = _moba_gate(proj, cos_s, sin_s, qg3, kg3, km_all, l,
                                       lay=lay, row0=MP, DB=DB, T=T, Hm=Hm)
        sel_flat = sel[:, :, :Hm * MOBA_TOPK].reshape(-1)
        o_c = _moba_sample(proj, o_c, q_rot, k_new, sel_flat, pt_flat, cache_kt, cache_vt, l,
                           lay=lay, row0=MP, DB=DB, T=T, Hm=Hm, n_pages=n_pages)
        x = _outproj_ffn(x, o_a, o_b, o_c, wo_a, wo_b, wo_c, norm_ffn3, wg, wu, wd, l, tm_ffn, th_ffn)

        g0, g1 = lay["GQ"], lay["GQ"] + 3 * GW
        outs[0].append(k_new.reshape(DB, T, Hm, HEAD_DIM))
        outs[1].append(proj[MP:, lay["MV"]:lay["MV"] + MW].reshape(DB, T, Hm, HEAD_DIM))
        outs[2].append(st_p)
        outs[3].append(st_s)
        outs[4].append(jnp.stack([proj[(b + 1) * S - (KWG - 1):(b + 1) * S, g0:g1] for b in range(B)]))
        hist_ext = jnp.concatenate([state_gdn_conv[l], proj[MP:, g0:g1].reshape(DB, T, 3 * GW)], axis=1)
        outs[5].append(hist_ext[:, -(KWG - 1):])
        outs[6].append(ch_p)
        outs[7].append(ch_s)

    y_prompt = x[:MP].reshape(B, S, D)
    y_sample = x[MP:].reshape(DB, T, D)
    k_prompt = jnp.transpose(kh_all, (0, 1, 3, 2, 4))
    v_prompt = jnp.transpose(vh_all, (0, 1, 3, 2, 4))
    return (y_prompt, y_sample, k_prompt, v_prompt) + tuple(jnp.stack(o) for o in outs)
```

```python
import functools

import jax
import jax.numpy as jnp
from jax import lax
from jax.experimental import pallas as pl
from jax.experimental.pallas import tpu as pltpu

F32 = jnp.float32
BF16 = jnp.bfloat16
I32 = jnp.int32

EPS = 1e-6
NEG_INF = -1e30
HEAD_DIM = 128
MOBA_BLOCK = 256
MOBA_TOPK = 3
ROPE_THETA = 10000.0
LANES = 128
SUBLANES = 8
BA_WIDTH = 2 * LANES
VMEM_LIMIT = 60 * 1024 * 1024


def _cp(sem, vmem=VMEM_LIMIT):
    return pltpu.CompilerParams(dimension_semantics=sem, vmem_limit_bytes=vmem)


def _bdot(a, b):
    return jnp.dot(a.astype(BF16), b.astype(BF16), preferred_element_type=F32)


def _nt(a, b):
    return lax.dot_general(a, b, (((1,), (1,)), ((), ())), preferred_element_type=F32)


def _bdot_nt(a, b):
    return _nt(a.astype(BF16), b.astype(BF16))


def _split_bf16(a):
    hi = a.astype(BF16)
    lo = (a - hi.astype(F32)).astype(BF16)
    return hi, lo


def _dot3_nt(a, b):
    ah, al = _split_bf16(a)
    bh, bl = _split_bf16(b)
    return _nt(ah, bh) + (_nt(ah, bl) + _nt(al, bh))


def _sigmoid(x):
    return 1.0 / (1.0 + jnp.exp(-x))


def _silu(x):
    return x * _sigmoid(x)


def _softplus(x):
    return jnp.maximum(x, 0.0) + jnp.log(1.0 + jnp.exp(-jnp.abs(x)))


def _rms(x, g):
    return x * lax.rsqrt(jnp.mean(x * x, axis=-1, keepdims=True) + EPS) * g


def _rope(x, cos_full, sin_signed):
    return x * cos_full + pltpu.roll(x, HEAD_DIM // 2, 1) * sin_signed


def _largest_divisor(n, cap, mult):
    best = None
    for d in range(mult, min(n, cap) + 1, mult):
        if n % d == 0:
            best = d
    assert best is not None, (n, cap, mult)
    return best


def _inproj_body(x_ref, g_ref, w_ref, o_ref, xn_ref):
    @pl.when(pl.program_id(1) == 0)
    def _():
        xn_ref[...] = _rms(x_ref[...], g_ref[...]).astype(BF16)

    o_ref[...] = jnp.dot(xn_ref[...], w_ref[...], preferred_element_type=F32)


def _inproj(x, g, w, l, tm, tn):
    M, D = x.shape
    NP = w.shape[-1]
    return pl.pallas_call(
        _inproj_body,
        grid=(M // tm, NP // tn),
        in_specs=[pl.BlockSpec((tm, D), lambda i, j: (i, 0)),
                  pl.BlockSpec((None, 1, D), lambda i, j: (l, 0, 0)),
                  pl.BlockSpec((None, D, tn), lambda i, j: (l, 0, j))],
        out_specs=pl.BlockSpec((tm, tn), lambda i, j: (i, j)),
        out_shape=jax.ShapeDtypeStruct((M, NP), F32),
        scratch_shapes=[pltpu.VMEM((tm, D), BF16)],
        compiler_params=_cp(("parallel", "arbitrary")),
        name="inproj",
    )(x, g, w)


def _ffn_body(x_ref, oa_ref, ob_ref, oc_ref, wa_ref, wb_ref, wc_ref, g_ref, wg_ref, wu_ref, wd_ref,
              y_ref, hn_ref):
    @pl.when(pl.program_id(1) == 0)
    def _():
        h = (x_ref[...]
             + jnp.dot(oa_ref[...], wa_ref[...], preferred_element_type=F32)
             + jnp.dot(ob_ref[...], wb_ref[...], preferred_element_type=F32)
             + jnp.dot(oc_ref[...], wc_ref[...], preferred_element_type=F32))
        y_ref[...] = h
        hn_ref[...] = _rms(h, g_ref[...]).astype(BF16)

    hn = hn_ref[...]
    a = jnp.dot(hn, wg_ref[...], preferred_element_type=F32)
    b = jnp.dot(hn, wu_ref[...], preferred_element_type=F32)
    t = (_silu(a) * b).astype(BF16)
    y_ref[...] += jnp.dot(t, wd_ref[...], preferred_element_type=F32)


def _outproj_ffn(x, oa, ob, oc, wa, wb, wc, g, wg, wu, wd, l, tm, th):
    M, D = x.shape
    FH = wg.shape[-1]
    GW, CC, MW = oa.shape[1], ob.shape[1], oc.shape[1]
    row = lambda i, j: (i, 0)
    return pl.pallas_call(
        _ffn_body,
        grid=(M // tm, FH // th),
        in_specs=[pl.BlockSpec((tm, D), row),
                  pl.BlockSpec((tm, GW), row),
                  pl.BlockSpec((tm, CC), row),
                  pl.BlockSpec((tm, MW), row),
                  pl.BlockSpec((None, GW, D), lambda i, j: (l, 0, 0), pipeline_mode=pl.Buffered(1)),
                  pl.BlockSpec((None, CC, D), lambda i, j: (l, 0, 0), pipeline_mode=pl.Buffered(1)),
                  pl.BlockSpec((None, MW, D), lambda i, j: (l, 0, 0), pipeline_mode=pl.Buffered(1)),
                  pl.BlockSpec((None, 1, D), lambda i, j: (l, 0, 0)),
                  pl.BlockSpec((None, D, th), lambda i, j: (l, 0, j)),
                  pl.BlockSpec((None, D, th), lambda i, j: (l, 0, j)),
                  pl.BlockSpec((None, th, D), lambda i, j: (l, j, 0))],
        out_specs=pl.BlockSpec((tm, D), row),
        out_shape=jax.ShapeDtypeStruct((M, D), F32),
        scratch_shapes=[pltpu.VMEM((tm, D), BF16)],
        compiler_params=_cp(("parallel", "arbitrary")),
        name="outproj_ffn",
    )(x, oa, ob, oc, wa, wb, wc, g, wg, wu, wd)


def _gdn_chunks(qs, ks, vs, betas, gs, Ss):
    nh = len(qs)
    hs = range(nh)
    C, dk = qs[0].shape
    assert C & (C - 1) == 0
    ri = lax.broadcasted_iota(I32, (C, C), 0)
    ci = lax.broadcasted_iota(I32, (C, C), 1)
    incl = ri >= ci
    eye = ri == ci
    xor = ri ^ ci
    g_row = [jnp.sum(jnp.where(eye, gs[h], 0.0), axis=0, keepdims=True) for h in hs]
    gc_row = [jnp.sum(jnp.where(ri <= ci, gs[h], 0.0), axis=0, keepdims=True) for h in hs]
    gc_col = [jnp.sum(jnp.where(incl, g_row[h], 0.0), axis=1, keepdims=True) for h in hs]
    decay = [jnp.where(incl, jnp.exp(jnp.where(incl, gc_col[h] - gc_row[h], 0.0)), 0.0) for h in hs]
    kbf = [ks[h].astype(BF16) for h in hs]
    kb = [ks[h] * betas[h] for h in hs]
    a = [jnp.where(ri > ci, _nt(kb[h].astype(BF16), kbf[h]) * decay[h], 0.0).astype(BF16) for h in hs]
    p = [jnp.where(eye, 1.0, 0.0) - jnp.where(xor == 1, a[h].astype(F32), 0.0) for h in hs]
    for lvl in range(1, C.bit_length() - 1):
        off = jnp.where((xor >> lvl) == 1, 1.0, 0.0).astype(BF16)
        pb = [p[h].astype(BF16) for h in hs]
        t1 = [jnp.dot(a[h] * off, pb[h], preferred_element_type=F32).astype(BF16) for h in hs]
        p = [p[h] - jnp.dot(pb[h], t1[h], preferred_element_type=F32) for h in hs]
    pb = [p[h].astype(BF16) for h in hs]
    egc = [jnp.exp(gc_col[h]) for h in hs]
    u = [jnp.dot(pb[h], (vs[h] * betas[h]).astype(BF16), preferred_element_type=F32) for h in hs]
    w = [jnp.dot(pb[h], (kb[h] * egc[h]).astype(BF16), preferred_element_type=F32) for h in hs]
    sb = [Ss[h].astype(BF16) for h in hs]
    v_new = [u[h] - jnp.dot(w[h].astype(BF16), sb[h], preferred_element_type=F32) for h in hs]
    vnb = [v_new[h].astype(BF16) for h in hs]
    qsc = [qs[h] * (dk ** -0.5) for h in hs]
    intra = [jnp.where(incl, _nt(qsc[h].astype(BF16), kbf[h]) * decay[h], 0.0).astype(BF16) for h in hs]
    o = [jnp.dot((qsc[h] * egc[h]).astype(BF16), sb[h], preferred_element_type=F32)
         + jnp.dot(intra[h], vnb[h], preferred_element_type=F32) for h in hs]
    g_last = [gc_col[h][C - 1:C, :] for h in hs]
    kdec_t = [(ks[h] * jnp.exp(g_last[h] - gc_col[h])).T.astype(BF16) for h in hs]
    s_new = [Ss[h] * jnp.exp(g_last[h]) + jnp.dot(kdec_t[h], vnb[h], preferred_element_type=F32) for h in hs]
    return o, s_new


def _gdn_body(q_ref, k_ref, v_ref, z_ref, ba_ref, hq_ref, hk_ref, hv_ref, s0_ref,
              wq_ref, wk_ref, wv_ref, alog_ref, dtb_ref, og_ref, obuf_hbm,
              o_ref, sout_ref, xq_ref, xk_ref, xv_ref, s_ref, *, Tt, C, HB, H, KW, NT):
    del obuf_hbm
    hb = pl.program_id(1)
    t = pl.program_id(2)
    HR = SUBLANES
    xrefs = (xq_ref, xk_ref, xv_ref)

    @pl.when(t == 0)
    def _():
        for xr, hr in zip(xrefs, (hq_ref, hk_ref, hv_ref)):
            xr[HR - (KW - 1):HR, :] = hr[...]
        s_ref[...] = s0_ref[...]

    for xr, r in zip(xrefs, (q_ref, k_ref, v_ref)):
        xr[HR:HR + Tt, :] = r[...]
        if C > Tt:
            xr[HR + Tt:HR + C, :] = jnp.zeros((C - Tt, xr.shape[1]), F32)

    def conv(xr, w_ref, s):
        acc = None
        for j in range(KW):
            term = xr[HR - (KW - 1) + j:HR - (KW - 1) + j + C, s * LANES:(s + 1) * LANES] \
                * w_ref[j:j + 1, s * LANES:(s + 1) * LANES]
            acc = term if acc is None else acc + term
        return acc

    ba = ba_ref[...]
    if C > Tt:
        ba = jnp.concatenate([ba, jnp.zeros((C - Tt, ba.shape[1]), F32)], axis=0)
    beta_all = _sigmoid(ba[:, :LANES])
    g_all = -jnp.exp(alog_ref[...]) * _softplus(ba[:, LANES:] + dtb_ref[...])
    lane = lax.broadcasted_iota(I32, (1, LANES), 1)
    rvalid = lax.broadcasted_iota(I32, (C, 1), 0) < Tt

    qs, ks, vs, betas, gs = [], [], [], [], []
    for s in range(HB):
        hsel = lane == hb * HB + s
        beta = jnp.sum(jnp.where(hsel, beta_all, 0.0), axis=1, keepdims=True)
        g = jnp.sum(jnp.where(hsel, g_all, 0.0), axis=1, keepdims=True)
        q = _silu(conv(xq_ref, wq_ref, s))
        k = _silu(conv(xk_ref, wk_ref, s))
        v = _silu(conv(xv_ref, wv_ref, s))
        q = q * lax.rsqrt(jnp.sum(q * q, axis=-1, keepdims=True) + EPS)
        k = k * lax.rsqrt(jnp.sum(k * k, axis=-1, keepdims=True) + EPS)
        if C > Tt:
            q = jnp.where(rvalid, q, 0.0)
            k = jnp.where(rvalid, k, 0.0)
            v = jnp.where(rvalid, v, 0.0)
            beta = jnp.where(rvalid, beta, 0.0)
            g = jnp.where(rvalid, g, 0.0)
        for lst, val in zip((qs, ks, vs, betas, gs), (q, k, v, beta, g)):
            lst.append(val)
    os_, s_new = _gdn_chunks(qs, ks, vs, betas, gs, [s_ref[s] for s in range(HB)])
    for s in range(HB):
        s_ref[s] = s_new[s]
        zz = z_ref[:, s * LANES:(s + 1) * LANES]
        o_ref[:, s * LANES:(s + 1) * LANES] = (_rms(os_[s][:Tt], og_ref[...]) * _silu(zz)).astype(o_ref.dtype)

    if NT > 1:
        for xr in xrefs:
            xr[0:HR, :] = xr[C:C + HR, :]

    @pl.when(t == NT - 1)
    def _():
        sout_ref[...] = s_ref[...]


def _gdn(proj, o_buf, hist, s0, conv_w, alog, dtb, og, l, *, lay, row0, nb, T, Tt, C, HB):
    M = proj.shape[0]
    H = s0.shape[1]
    GW = H * HEAD_DIM
    KW = conv_w.shape[1]
    NT = T // Tt
    W = HB * LANES
    assert T % Tt == 0 and row0 % Tt == 0 and H % HB == 0 and (NT == 1 or Tt == C)
    rb0 = row0 // Tt

    def col(off):
        assert off % W == 0
        return lambda b, h, t: (rb0 + b * NT + t, off // W + h)

    def hcol(off):
        return lambda b, h, t: (b, 0, off // W + h)

    def wcol(off):
        return lambda b, h, t: (l, 0, off // W + h)

    vec = lambda b, h, t: (l, 0, 0)
    in_specs = [pl.BlockSpec((Tt, W), col(lay["GQ"])),
                pl.BlockSpec((Tt, W), col(lay["GK"])),
                pl.BlockSpec((Tt, W), col(lay["GV"])),
                pl.BlockSpec((Tt, W), col(lay["GZ"])),
                pl.BlockSpec((Tt, BA_WIDTH), lambda b, h, t: (rb0 + b * NT + t, lay["BA"] // BA_WIDTH)),
                pl.BlockSpec((None, KW - 1, W), hcol(0)),
                pl.BlockSpec((None, KW - 1, W), hcol(GW)),
                pl.BlockSpec((None, KW - 1, W), hcol(2 * GW)),
                pl.BlockSpec((None, HB, HEAD_DIM, HEAD_DIM), lambda b, h, t: (b, h, 0, 0)),
                pl.BlockSpec((None, KW, W), wcol(0)),
                pl.BlockSpec((None, KW, W), wcol(GW)),
                pl.BlockSpec((None, KW, W), wcol(2 * GW)),
                pl.BlockSpec((None, 1, LANES), vec),
                pl.BlockSpec((None, 1, LANES), vec),
                pl.BlockSpec((None, 1, HEAD_DIM), vec),
                pl.BlockSpec(memory_space=pl.ANY)]
    args = [proj, proj, proj, proj, proj, hist, hist, hist, s0, conv_w, conv_w, conv_w, alog, dtb, og, o_buf]
    assert o_buf.shape == (M, GW) and o_buf.dtype == BF16
    o, st = pl.pallas_call(
        functools.partial(_gdn_body, Tt=Tt, C=C, HB=HB, H=H, KW=KW, NT=NT),
        grid=(nb, H // HB, NT),
        in_specs=in_specs,
        out_specs=[pl.BlockSpec((Tt, W), lambda b, h, t: (rb0 + b * NT + t, h)),
                   pl.BlockSpec((None, HB, HEAD_DIM, HEAD_DIM), lambda b, h, t: (b, h, 0, 0))],
        out_shape=[jax.ShapeDtypeStruct((M, GW), BF16),
                   jax.ShapeDtypeStruct((nb, H, HEAD_DIM, HEAD_DIM), F32)],
        scratch_shapes=[pltpu.VMEM((SUBLANES + C, W), F32)] * 3 + [pltpu.VMEM((HB, HEAD_DIM, HEAD_DIM), F32)],
        input_output_aliases={len(args) - 1: 0},
        compiler_params=_cp(("parallel", "parallel", "arbitrary")),
        name="gdn",
    )(*args)
    return o, st


CONF_HDR = 32
CONF_SUB = 32


def _conf_body(a_ref, gt_ref, hist_ref, w_ref, b_ref, lg_ref, lb_ref, obuf_hbm, o_ref, hout_ref, buf_ref, sh_ref,
               *, Tt, KW, NT):
    del obuf_hbm
    t = pl.program_id(1)
    HR = CONF_HDR
    base = HR - (KW - 1)

    @pl.when(t == 0)
    def _():
        buf_ref[0:HR, :] = jnp.zeros((HR, buf_ref.shape[1]), F32)
        buf_ref[base:HR, :] = hist_ref[...]

    buf_ref[HR:HR + Tt, :] = a_ref[...] * _sigmoid(gt_ref[...])
    span = HR + Tt - SUBLANES
    for r in range(1, SUBLANES):
        sh_ref[r - 1, 0:span, :] = buf_ref[r:r + span, :]

    def tap_rows(j, r0, n):
        a, r = divmod(base + j, SUBLANES)
        lo = r0 + a * SUBLANES
        return buf_ref[lo:lo + n, :] if r == 0 else sh_ref[r - 1, lo:lo + n, :]

    sub = min(CONF_SUB, Tt)
    for r0 in range(0, Tt, sub):
        acc = None
        for j in range(KW):
            term = tap_rows(j, r0, sub) * w_ref[j:j + 1, :]
            acc = term if acc is None else acc + term
        y = acc + b_ref[...]
        yc = y - jnp.mean(y, axis=-1, keepdims=True)
        yn = yc * lax.rsqrt(jnp.mean(yc * yc, axis=-1, keepdims=True) + EPS) * lg_ref[...] + lb_ref[...]
        o_ref[r0:r0 + sub, :] = _silu(yn).astype(o_ref.dtype)

    @pl.when(t == NT - 1)
    def _():
        hout_ref[...] = buf_ref[Tt + base:Tt + HR, :]

    if NT > 1:
        buf_ref[0:HR, :] = buf_ref[Tt:Tt + HR, :]


def _conformer(proj, o_buf, hist, dw_w, dw_b, ln_g, ln_b, l, *, lay, row0, nb, T, Tt):
    M = proj.shape[0]
    CC = dw_w.shape[2]
    KW = dw_w.shape[1]
    NT = T // Tt
    assert T % Tt == 0 and row0 % Tt == 0 and lay["GLU"] % CC == 0 and KW - 1 <= CONF_HDR
    assert NT == 1 or Tt >= CONF_HDR
    rb0 = row0 // Tt
    cb = lay["GLU"] // CC
    vec = lambda b, t: (l, 0, 0)
    in_specs = [pl.BlockSpec((Tt, CC), lambda b, t: (rb0 + b * NT + t, cb)),
                pl.BlockSpec((Tt, CC), lambda b, t: (rb0 + b * NT + t, cb + 1)),
                pl.BlockSpec((None, KW - 1, CC), lambda b, t: (b, 0, 0)),
                pl.BlockSpec((None, KW, CC), vec),
                pl.BlockSpec((None, 1, CC), vec),
                pl.BlockSpec((None, 1, CC), vec),
                pl.BlockSpec((None, 1, CC), vec),
                pl.BlockSpec(memory_space=pl.ANY)]
    args = [proj, proj, hist, dw_w, dw_b, ln_g, ln_b, o_buf]
    assert o_buf.shape == (M, CC) and o_buf.dtype == BF16
    o, hn = pl.pallas_call(
        functools.partial(_conf_body, Tt=Tt, KW=KW, NT=NT),
        grid=(nb, NT),
        in_specs=in_specs,
        out_specs=[pl.BlockSpec((Tt, CC), lambda b, t: (rb0 + b * NT + t, 0)),
                   pl.BlockSpec((None, KW - 1, CC), lambda b, t: (b, 0, 0))],
        out_shape=[jax.ShapeDtypeStruct((M, CC), BF16),
                   jax.ShapeDtypeStruct((nb, KW - 1, CC), F32)],
        scratch_shapes=[pltpu.VMEM((CONF_HDR + Tt, CC), F32),
                        pltpu.VMEM((SUBLANES - 1, CONF_HDR + Tt, CC), F32)],
        input_output_aliases={len(args) - 1: 0},
        compiler_params=_cp(("parallel", "arbitrary")),
        name="conformer",
    )(*args)
    return o, hn


def _moba_prompt_body(q_ref, k_ref, v_ref, cq_ref, sq_ref, ck_ref, sk_ref, qg_ref, kg_ref,
                      obuf_hbm, khbuf_hbm, vhbuf_hbm,
                      o_ref, kh_ref, vh_ref, kbf_ref, vt_ref, km_ref, sel_ref, *, NB, HPS):
    del obuf_hbm, khbuf_hbm, vhbuf_hbm
    i = pl.program_id(2)
    BK = MOBA_BLOCK
    hd = HEAD_DIM
    heads = range(HPS)

    @pl.when(i == 0)
    def _():
        def blk(n, carry):
            r = pl.multiple_of(n * BK, BK)
            cos_b, sin_b = ck_ref[pl.ds(r, BK), :], sk_ref[pl.ds(r, BK), :]
            eye_bf = jnp.where(lax.broadcasted_iota(I32, (hd, hd), 0) == lax.broadcasted_iota(I32, (hd, hd), 1),
                               1.0, 0.0).astype(BF16)
            for hh in heads:
                cs = slice(hh * hd, (hh + 1) * hd)
                kr = _rope(_rms(k_ref[pl.ds(r, BK), cs], kg_ref[...]), cos_b, sin_b)
                vb = v_ref[pl.ds(r, BK), cs]
                kh_ref[hh, pl.ds(r, BK), :] = kr
                vh_ref[hh, pl.ds(r, BK), :] = vb
                kbf_ref[hh, n] = kr.astype(BF16)
                vt_ref[hh, n] = _nt(eye_bf, vb.astype(BF16)).astype(BF16)
                km_ref[hh, pl.ds(n, 1), :] = jnp.sum(kr, axis=0, keepdims=True) * (1.0 / BK)
            return carry
        lax.fori_loop(0, NB, blk, 0)

    qs = []
    for hh in heads:
        qr = _rope(_rms(q_ref[:, hh * hd:(hh + 1) * hd], qg_ref[...]), cq_ref[...], sq_ref[...])
        gate = _dot3_nt(km_ref[hh], qr)
        nidx = lax.broadcasted_iota(I32, gate.shape, 0)
        past = nidx < i
        gm = jnp.where(past, gate, NEG_INF)
        rank = jnp.zeros(gate.shape, I32)
        for m in range(NB):
            grow = gm[m:m + 1, :]
            beats = (grow > gm) | ((grow == gm) & (nidx > m))
            rank = rank + jnp.where(beats, 1, 0)
        sel_ref[hh] = jnp.where(past & (rank < MOBA_TOPK), 1.0, 0.0)
        qs.append((qr * (hd ** -0.5)).astype(BF16))

    def update(carry, s_list, vt_list):
        m_prev, l_prev, acc = carry
        mx = None
        for s in s_list:
            cur = jnp.max(s, axis=0, keepdims=True)
            mx = cur if mx is None else jnp.maximum(mx, cur)
        m_new = jnp.maximum(m_prev, mx)
        alpha = jnp.exp(m_prev - m_new)
        l_new = alpha * l_prev
        acc = alpha * acc
        for s, vt in zip(s_list, vt_list):
            p = jnp.exp(s - m_new)
            l_new = l_new + jnp.sum(p, axis=0, keepdims=True)
            acc = acc + jnp.dot(vt, p.astype(BF16), preferred_element_type=F32)
        return m_new, l_new, acc

    def update_all(carries, s_all, vt_all):
        m_new, alpha, p_all = [], [], []
        for hh in heads:
            mx = None
            for s in s_all[hh]:
                cur = jnp.max(s, axis=0, keepdims=True)
                mx = cur if mx is None else jnp.maximum(mx, cur)
            m_new.append(jnp.maximum(carries[hh][0], mx))
            alpha.append(jnp.exp(carries[hh][0] - m_new[hh]))
        for hh in heads:
            p_all.append([jnp.exp(s - m_new[hh]) for s in s_all[hh]])
        out = []
        for hh in heads:
            l_new = alpha[hh] * carries[hh][1]
            acc = alpha[hh] * carries[hh][2]
            for p, vt in zip(p_all[hh], vt_all[hh]):
                l_new = l_new + jnp.sum(p, axis=0, keepdims=True)
                acc = acc + jnp.dot(vt, p.astype(BF16), preferred_element_type=F32)
            out.append((m_new[hh], l_new, acc))
        return tuple(out)

    def group(n0, width, carries):
        s_all = [[jnp.where(sel_ref[hh, pl.ds(n0 + d, 1), :] > 0.5,
                            _nt(kbf_ref[hh, n0 + d], qs[hh]), NEG_INF)
                  for d in range(width)] for hh in heads]
        return update_all(carries, s_all, [[vt_ref[hh, n0 + d] for d in range(width)] for hh in heads])

    init = tuple((jnp.full((1, BK), NEG_INF, F32), jnp.zeros((1, BK), F32), jnp.zeros((hd, BK), F32))
                 for _ in heads)
    gw = 4 // HPS if HPS <= 2 else 1
    nwide = i // gw if gw > 1 else 0
    carries = init
    if gw > 1:
        carries = lax.fori_loop(0, nwide, lambda g, c: group(gw * g, gw, c), carries)
    rem = i - gw * nwide if gw > 1 else i
    if gw == 4:
        carries = lax.fori_loop(0, (rem + 1) // 2, lambda g, c: group(gw * nwide + 2 * g, 2, c), carries)
    else:
        carries = lax.fori_loop(0, rem, lambda g, c: group(gw * nwide + g, 1, c), carries)
    kpos = lax.broadcasted_iota(I32, (BK, BK), 0)
    qpos = lax.broadcasted_iota(I32, (BK, BK), 1)
    s_own = [jnp.where(kpos <= qpos, _nt(kbf_ref[hh, i], qs[hh]), NEG_INF) for hh in heads]
    for hh in heads:
        _, l_fin, acc = update(carries[hh], [s_own[hh]], [vt_ref[hh, i]])
        o_ref[:, hh * hd:(hh + 1) * hd] = (acc / l_fin).T.astype(o_ref.dtype)


def _moba_prompt(proj, o_buf, kh_buf, vh_buf, cos_t, sin_t, qg, kg, l, *, lay, B, S, Hm):
    M = proj.shape[0]
    BK = MOBA_BLOCK
    assert S % BK == 0
    NB = S // BK
    MW = Hm * HEAD_DIM
    hd = HEAD_DIM
    vec = lambda b, h, i: (l, 0, 0)
    L = kh_buf.shape[0]
    assert kh_buf.shape == vh_buf.shape == (L, B, Hm, S, hd) and kh_buf.dtype == vh_buf.dtype == F32
    HPS = 2 if Hm % 2 == 0 else 1
    W = HPS * hd
    assert lay["MQ"] % W == 0 and lay["MK"] % W == 0 and lay["MV"] % W == 0
    kv_out = pl.BlockSpec((None, None, HPS, S, hd), lambda b, h, i: (l, b, h, 0, 0))
    return pl.pallas_call(
        functools.partial(_moba_prompt_body, NB=NB, HPS=HPS),
        grid=(B, Hm // HPS, NB),
        in_specs=[pl.BlockSpec((BK, W), lambda b, h, i: (b * NB + i, lay["MQ"] // W + h)),
                  pl.BlockSpec((S, W), lambda b, h, i: (b, lay["MK"] // W + h)),
                  pl.BlockSpec((S, W), lambda b, h, i: (b, lay["MV"] // W + h)),
                  pl.BlockSpec((BK, hd), lambda b, h, i: (i, 0)),
                  pl.BlockSpec((BK, hd), lambda b, h, i: (i, 0)),
                  pl.BlockSpec((S, hd), lambda b, h, i: (0, 0)),
                  pl.BlockSpec((S, hd), lambda b, h, i: (0, 0)),
                  pl.BlockSpec((None, 1, hd), vec),
                  pl.BlockSpec((None, 1, hd), vec),
                  pl.BlockSpec(memory_space=pl.ANY),
                  pl.BlockSpec(memory_space=pl.ANY),
                  pl.BlockSpec(memory_space=pl.ANY)],
        out_specs=[pl.BlockSpec((BK, W), lambda b, h, i: (b * NB + i, h)), kv_out, kv_out],
        out_shape=[jax.ShapeDtypeStruct((M, MW), BF16),
                   jax.ShapeDtypeStruct(kh_buf.shape, F32),
                   jax.ShapeDtypeStruct(vh_buf.shape, F32)],
        scratch_shapes=[pltpu.VMEM((HPS, NB, BK, hd), BF16), pltpu.VMEM((HPS, NB, hd, BK), BF16),
                        pltpu.VMEM((HPS, NB, hd), F32), pltpu.VMEM((HPS, NB, BK), F32)],
        input_output_aliases={9: 0, 10: 1, 11: 2},
        compiler_params=_cp(("parallel", "parallel", "arbitrary")),
        name="moba_prompt",
    )(proj, proj, proj, cos_t, sin_t, cos_t, sin_t, qg, kg, o_buf, kh_buf, vh_buf)


KM_GROUP = 8


def _kmeans_body(pt_ref, *refs, PPB, PAGE):
    ins, o_ref = refs[:-1], refs[-1]
    for g in range(KM_GROUP):
        s = None
        for p in range(PPB):
            part = jnp.sum(ins[g * PPB + p][...], axis=1)
            s = part if s is None else s + part
        o_ref[g] = s * (1.0 / (PPB * PAGE))


def _cache_block_means(cache_k, pt_flat, *, DB, n_pages):
    L, _, Hm, PAGE, hd = cache_k.shape
    PPB = MOBA_BLOCK // PAGE
    NBK = n_pages // PPB
    assert MOBA_BLOCK % PAGE == 0 and n_pages % PPB == 0 and NBK % KM_GROUP == 0
    PG = KM_GROUP * PPB

    def page_spec(p):
        return pl.BlockSpec((None, None, Hm, PAGE, hd),
                            lambda l, b, g, pt: (l, pt[b * n_pages + g * PG + p], 0, 0, 0))

    return pl.pallas_call(
        functools.partial(_kmeans_body, PPB=PPB, PAGE=PAGE),
        grid_spec=pltpu.PrefetchScalarGridSpec(
            num_scalar_prefetch=1,
            grid=(L, DB, NBK // KM_GROUP),
            in_specs=[page_spec(p) for p in range(PG)],
            out_specs=pl.BlockSpec((None, None, KM_GROUP, Hm, hd), lambda l, b, g, pt: (l, b, g, 0, 0))),
        out_shape=jax.ShapeDtypeStruct((L, DB, NBK, Hm, hd), F32),
        compiler_params=_cp(("parallel", "parallel", "arbitrary")),
        name="cache_block_means",
    )(pt_flat, *([cache_k] * PG))


def _moba_gate_body(qkv_ref, cos_ref, sin_ref, qg_ref, kg_ref, km_ref, q_ref, k_ref, sel_ref, *, Hm, NBK):
    MW = Hm * HEAD_DIM
    T = qkv_ref.shape[0]
    lane_o = lax.broadcasted_iota(I32, (T, LANES), 1)
    lane_g = lax.broadcasted_iota(I32, (T, NBK), 1)
    sel = jnp.zeros((T, LANES), I32)
    for h in range(Hm):
        c0 = h * HEAD_DIM
        qr = _rope(_rms(qkv_ref[:, c0:c0 + HEAD_DIM], qg_ref[...]), cos_ref[...], sin_ref[...])
        kr = _rope(_rms(qkv_ref[:, MW + c0:MW + c0 + HEAD_DIM], kg_ref[...]), cos_ref[...], sin_ref[...])
        q_ref[:, c0:c0 + HEAD_DIM] = qr
        k_ref[:, c0:c0 + HEAD_DIM] = kr
        gate = _dot3_nt(qr, km_ref[:, h, :])
        for j in range(MOBA_TOPK):
            mx = jnp.max(gate, axis=1, keepdims=True)
            idx = jnp.min(jnp.where(gate == mx, lane_g, NBK), axis=1, keepdims=True)
            sel = jnp.where(lane_o == h * MOBA_TOPK + j, idx, sel)
            gate = jnp.where(lane_g == idx, -jnp.inf, gate)
    sel_ref[...] = sel


def _moba_gate(proj, cos_s, sin_s, qg, kg, km, l, *, lay, row0, DB, T, Hm):
    MW = Hm * HEAD_DIM
    NBK = km.shape[2]
    assert row0 % T == 0 and lay["MQ"] == 0 and lay["MK"] == MW and Hm * MOBA_TOPK <= LANES and NBK >= MOBA_TOPK
    vec = lambda b: (l, 0, 0)
    return pl.pallas_call(
        functools.partial(_moba_gate_body, Hm=Hm, NBK=NBK),
        grid=(DB,),
        in_specs=[pl.BlockSpec((T, 3 * MW), lambda b: (row0 // T + b, 0)),
                  pl.BlockSpec((T, HEAD_DIM), lambda b: (0, 0)),
                  pl.BlockSpec((T, HEAD_DIM), lambda b: (0, 0)),
                  pl.BlockSpec((None, 1, HEAD_DIM), vec),
                  pl.BlockSpec((None, 1, HEAD_DIM), vec),
                  pl.BlockSpec((None, None, NBK, Hm, HEAD_DIM), lambda b: (l, b, 0, 0, 0))],
        out_specs=[pl.BlockSpec((T, MW), lambda b: (b, 0)),
                   pl.BlockSpec((T, MW), lambda b: (b, 0)),
                   pl.BlockSpec((None, T, LANES), lambda b: (b, 0, 0))],
        out_shape=[jax.ShapeDtypeStruct((DB * T, MW), F32),
                   jax.ShapeDtypeStruct((DB * T, MW), F32),
                   jax.ShapeDtypeStruct((DB, T, LANES), I32)],
        compiler_params=_cp(("parallel",)),
        name="moba_gate",
    )(proj, cos_s, sin_s, qg, kg, km)


def _moba_sample_body(pt_ref, sel_ref, q_ref, k_ref, v_ref, ck_hbm, cv_hbm, obuf_hbm, o_ref,
                      kbuf_ref, vbuf_ref, sem_ref, *, l, T, Hm, n_pages, PPB, PAGE):
    del obuf_hbm
    b = pl.program_id(0)
    BK = MOBA_BLOCK
    NS = T * MOBA_TOPK
    hd = HEAD_DIM

    def copies(h, buf):
        out = []
        for slot in range(NS):
            t, j = divmod(slot, MOBA_TOPK)
            blk = sel_ref[((b * T + t) * Hm + h) * MOBA_TOPK + j]
            for p in range(PPB):
                page = pt_ref[b * n_pages + blk * PPB + p]
                dst = pl.ds(slot * BK + p * PAGE, PAGE)
                out.append(pltpu.make_async_copy(ck_hbm.at[l, page, h], kbuf_ref.at[buf, dst, :], sem_ref.at[0, buf]))
                out.append(pltpu.make_async_copy(cv_hbm.at[l, page, h], vbuf_ref.at[buf, dst, :], sem_ref.at[1, buf]))
        return out

    TP = 2 * SUBLANES
    zpad = jnp.zeros((TP - T, hd), F32)
    span = MOBA_TOPK * BK

    def attend(h, buf):
        c0 = h * hd
        qs = jnp.concatenate([q_ref[:, c0:c0 + hd] * (hd ** -0.5), zpad], axis=0).astype(BF16)
        kn = jnp.concatenate([k_ref[:, c0:c0 + hd], zpad], axis=0).astype(BF16)
        vn = jnp.concatenate([v_ref[:, c0:c0 + hd], zpad], axis=0).astype(BF16)
        s_p = _nt(qs, kbuf_ref[buf].astype(BF16))
        ri = lax.broadcasted_iota(I32, s_p.shape, 0)
        ci = lax.broadcasted_iota(I32, s_p.shape, 1)
        s_p = jnp.where((ci >= ri * span) & (ci < (ri + 1) * span), s_p, NEG_INF)
        s_o = _nt(qs, kn)
        ro = lax.broadcasted_iota(I32, s_o.shape, 0)
        co = lax.broadcasted_iota(I32, s_o.shape, 1)
        s_o = jnp.where((co <= ro) & (co < T), s_o, NEG_INF)
        m = jnp.maximum(jnp.max(s_p, axis=1, keepdims=True), jnp.max(s_o, axis=1, keepdims=True))
        p_p = jnp.exp(s_p - m)
        p_o = jnp.exp(s_o - m)
        den = jnp.sum(p_p, axis=1, keepdims=True) + jnp.sum(p_o, axis=1, keepdims=True)
        o = (jnp.dot(p_p.astype(BF16), vbuf_ref[buf].astype(BF16), preferred_element_type=F32)
             + jnp.dot(p_o.astype(BF16), vn, preferred_element_type=F32)) / den
        o_ref[:, c0:c0 + hd] = o[:T].astype(o_ref.dtype)

    for c in copies(0, 0):
        c.start()
    for h in range(Hm):
        buf = h % 2
        if h + 1 < Hm:
            for c in copies(h + 1, 1 - buf):
                c.start()
        for c in copies(h, buf):
            c.wait()
        attend(h, buf)


def _moba_sample(proj, o_buf, q_rot, k_new, sel_flat, pt_flat, cache_k, cache_v, l, *, lay, row0, DB, T, Hm, n_pages):
    PAGE = cache_k.shape[3]
    PPB = MOBA_BLOCK // PAGE
    NS = T * MOBA_TOPK
    hd = HEAD_DIM
    MW = Hm * hd
    assert T <= 2 * SUBLANES and row0 % T == 0 and lay["MV"] % MW == 0 and cache_k.shape[2] == Hm
    grid_spec = pltpu.PrefetchScalarGridSpec(
        num_scalar_prefetch=2,
        grid=(DB,),
        in_specs=[pl.BlockSpec((T, MW), lambda b, pt, sl: (b, 0)),
                  pl.BlockSpec((T, MW), lambda b, pt, sl: (b, 0)),
                  pl.BlockSpec((T, MW), lambda b, pt, sl: (row0 // T + b, lay["MV"] // MW)),
                  pl.BlockSpec(memory_space=pl.ANY),
                  pl.BlockSpec(memory_space=pl.ANY),
                  pl.BlockSpec(memory_space=pl.ANY)],
        out_specs=pl.BlockSpec((T, MW), lambda b, pt, sl: (row0 // T + b, 0)),
        scratch_shapes=[pltpu.VMEM((2, NS * MOBA_BLOCK, hd), F32), pltpu.VMEM((2, NS * MOBA_BLOCK, hd), F32),
                        pltpu.SemaphoreType.DMA((2, 2))])
    return pl.pallas_call(
        functools.partial(_moba_sample_body, l=l, T=T, Hm=Hm, n_pages=n_pages, PPB=PPB, PAGE=PAGE),
        grid_spec=grid_spec,
        out_shape=jax.ShapeDtypeStruct(o_buf.shape, o_buf.dtype),
        input_output_aliases={7: 0},
        compiler_params=_cp(("arbitrary",)),
        name="moba_sample",
    )(pt_flat, sel_flat, q_rot, k_new, proj, cache_k, cache_v, o_buf)


def _layout(GW, MW, CC):
    def up(x, m):
        return (x + m - 1) // m * m
    lay = {"MQ": 0, "MK": MW, "MV": 2 * MW, "GQ": 3 * MW, "GK": 3 * MW + GW, "GV": 3 * MW + 2 * GW,
           "GZ": 3 * MW + 3 * GW}
    lay["BA"] = up(lay["GZ"] + GW, BA_WIDTH)
    lay["GLU"] = up(lay["BA"] + BA_WIDTH, CC)
    lay["NP"] = up(lay["GLU"] + 2 * CC, 4 * LANES)
    return lay


def _arrange_w_in(w_in, lay, GW, MW, CC, Hg):
    L, D, _ = w_in.shape
    o = 0
    qkv_a = w_in[..., o:o + 3 * GW]; o += 3 * GW
    z_a = w_in[..., o:o + GW]; o += GW
    b_a = w_in[..., o:o + Hg]; o += Hg
    a_a = w_in[..., o:o + Hg]; o += Hg
    glu = w_in[..., o:o + 2 * CC]; o += 2 * CC
    qkv_c = w_in[..., o:o + 3 * MW]; o += 3 * MW
    assert o == w_in.shape[-1]
    z = lambda n: jnp.zeros((L, D, n), w_in.dtype)
    parts = [qkv_c, qkv_a, z_a, z(lay["BA"] - lay["GZ"] - GW), b_a, z(LANES - Hg), a_a, z(LANES - Hg),
             z(lay["GLU"] - lay["BA"] - BA_WIDTH), glu, z(lay["NP"] - lay["GLU"] - 2 * CC)]
    return jnp.concatenate(parts, axis=-1).astype(BF16)


def _rope_tables(pos):
    half = HEAD_DIM // 2
    inv_freq = ROPE_THETA ** (-jnp.arange(half, dtype=F32) / half)
    ang = pos.astype(F32)[:, None] * inv_freq[None, :]
    cos, sin = jnp.cos(ang), jnp.sin(ang)
    return jnp.concatenate([cos, cos], axis=-1), jnp.concatenate([-sin, sin], axis=-1)


def _pad_lanes(a):
    L, H = a.shape
    return jnp.pad(a, ((0, 0), (0, LANES - H))).reshape(L, 1, LANES)


def kernel(x_prompt, x_sample, cache_k, cache_v, state_gdn, state_gdn_conv, state_conv, page_table, norm_mix, w_in, gdn_conv_w, gdn_a_log, gdn_dt_bias, gdn_out_norm, conv_dw_w, conv_dw_b, conv_ln_g, conv_ln_b, moba_q_norm, moba_k_norm, w_out, norm_ffn, w_gate, w_up, w_down):
    B, S, D = x_prompt.shape
    DB, T, _ = x_sample.shape
    L = w_in.shape[0]
    Hg = gdn_a_log.shape[1]
    Hm = cache_k.shape[3]
    PAGE = cache_k.shape[2]
    n_pool = cache_k.shape[1]
    n_pages = page_table.shape[1]
    CC = conv_dw_b.shape[1]
    KWG = gdn_conv_w.shape[1]
    KWC = conv_dw_w.shape[1]
    FH = w_gate.shape[2]
    GW, MW = Hg * HEAD_DIM, Hm * HEAD_DIM
    past_len = n_pages * PAGE
    assert cache_k.shape[4] == HEAD_DIM and past_len % MOBA_BLOCK == 0 and GW + CC + MW == D
    assert (past_len + T - 1) // MOBA_BLOCK == past_len // MOBA_BLOCK
    lay = _layout(GW, MW, CC)
    NP = lay["NP"]
    MP, MS = B * S, DB * T
    M = MP + MS

    w_in_r = _arrange_w_in(w_in, lay, GW, MW, CC, Hg)
    wo = w_out.astype(BF16)
    wo_a, wo_b, wo_c = wo[:, :GW], wo[:, GW:GW + CC], wo[:, GW + CC:]
    wg, wu, wd = w_gate.astype(BF16), w_up.astype(BF16), w_down.astype(BF16)
    vec3 = lambda a: a.reshape(L, 1, a.shape[-1])
    norm_mix3, norm_ffn3 = vec3(norm_mix), vec3(norm_ffn)
    og3, qg3, kg3 = vec3(gdn_out_norm), vec3(moba_q_norm), vec3(moba_k_norm)
    dwb3, lng3, lnb3 = vec3(conv_dw_b), vec3(conv_ln_g), vec3(conv_ln_b)
    alog3, dtb3 = _pad_lanes(gdn_a_log), _pad_lanes(gdn_dt_bias)
    cos_p, sin_p = _rope_tables(jnp.arange(S, dtype=I32))
    cos_s, sin_s = _rope_tables(past_len + jnp.arange(T, dtype=I32))
    pt_flat = page_table.reshape(-1).astype(I32)
    gdn_hist0 = jnp.zeros((B, KWG - 1, 3 * GW), F32)
    gdn_state0 = jnp.zeros((B, Hg, HEAD_DIM, HEAD_DIM), F32)
    conv_hist0 = jnp.zeros((B, KWC - 1, CC), F32)

    tm_in = _largest_divisor(M, 1400, SUBLANES)
    tn_in = _largest_divisor(NP, 512, LANES)
    tm_ffn = _largest_divisor(M, 700, SUBLANES)
    th_ffn = _largest_divisor(FH, 512, LANES)
    gdn_c = MOBA_BLOCK if S % MOBA_BLOCK == 0 else _largest_divisor(S, 256, SUBLANES)
    conf_t = _largest_divisor(S, 256, CONF_HDR)

    cache_kt = jnp.transpose(cache_k, (0, 1, 3, 2, 4))
    cache_vt = jnp.transpose(cache_v, (0, 1, 3, 2, 4))
    km_all = _cache_block_means(cache_kt, pt_flat, DB=DB, n_pages=n_pages)

    x = jnp.concatenate([x_prompt.reshape(MP, D), x_sample.reshape(MS, D)], axis=0)
    o_a = jnp.zeros((M, GW), BF16)
    o_b = jnp.zeros((M, CC), BF16)
    o_c = jnp.zeros((M, MW), BF16)
    kh_all = jnp.zeros((L, B, Hm, S, HEAD_DIM), F32)
    vh_all = jnp.zeros((L, B, Hm, S, HEAD_DIM), F32)
    outs = [[] for _ in range(8)]
    for l in range(L):
        proj = _inproj(x, norm_mix3, w_in_r, l, tm_in, tn_in)
        o_a, st_p = _gdn(proj, o_a, gdn_hist0, gdn_state0, gdn_conv_w, alog3, dtb3, og3, l,
                         lay=lay, row0=0, nb=B, T=S, Tt=gdn_c, C=gdn_c, HB=Hg)
        o_a, st_s = _gdn(proj, o_a, state_gdn_conv[l], state_gdn[l], gdn_conv_w, alog3, dtb3, og3, l,
                         lay=lay, row0=MP, nb=DB, T=T, Tt=T, C=LANES, HB=Hg)
        o_b, ch_p = _conformer(proj, o_b, conv_hist0, conv_dw_w, dwb3, lng3, lnb3, l,
                               lay=lay, row0=0, nb=B, T=S, Tt=conf_t)
        o_b, ch_s = _conformer(proj, o_b, state_conv[l], conv_dw_w, dwb3, lng3, lnb3, l,
                               lay=lay, row0=MP, nb=DB, T=T, Tt=T)
        o_c, kh_all, vh_all = _moba_prompt(proj, o_c, kh_all, vh_all, cos_p, sin_p, qg3, kg3, l,
                                           lay=lay, B=B, S=S, Hm=Hm)
        q_rot, k_new, sel = _moba_gate(proj, cos_s, sin_s, qg3, kg3, km_all, l,
                                       lay=lay, row0=MP, DB=DB, T=T, Hm=Hm)
        sel_flat = sel[:, :, :Hm * MOBA_TOPK].reshape(-1)
        o_c = _moba_sample(proj, o_c, q_rot, k_new, sel_flat, pt_flat, cache_kt, cache_vt, l,
                           lay=lay, row0=MP, DB=DB, T=T, Hm=Hm, n_pages=n_pages)
        x = _outproj_ffn(x, o_a, o_b, o_c, wo_a, wo_b, wo_c, norm_ffn3, wg, wu, wd, l, tm_ffn, th_ffn)

        g0, g1 = lay["GQ"], lay["GQ"] + 3 * GW
        outs[0].append(k_new.reshape(DB, T, Hm, HEAD_DIM))
        outs[1].append(proj[MP:, lay["MV"]:lay["MV"] + MW].reshape(DB, T, Hm, HEAD_DIM))
        outs[2].append(st_p)
        outs[3].append(st_s)
        outs[4].append(jnp.stack([proj[(b + 1) * S - (KWG - 1):(b + 1) * S, g0:g1] for b in range(B)]))
        hist_ext = jnp.concatenate([state_gdn_conv[l], proj[MP:, g0:g1].reshape(DB, T, 3 * GW)], axis=1)
        outs[5].append(hist_ext[:, -(KWG - 1):])
        outs[6].append(ch_p)
        outs[7].append(ch_s)

    y_prompt = x[:MP].reshape(B, S, D)
    y_sample = x[MP:].reshape(DB, T, D)
    k_prompt = jnp.transpose(kh_all, (0, 1, 3, 2, 4))
    v_prompt = jnp.transpose(vh_all, (0, 1, 3, 2, 4))
    return (y_prompt, y_sample, k_prompt, v_prompt) + tuple(jnp.stack(o) for o in outs)
```
